```python
import math
import jax, jax.numpy as jnp
from jax import lax
import numpy as np

D_MODEL = 1024
BATCH = 16
SEQ = 256
DEPTH = 4
DEC_BATCH = 8
DEC_SEQ = 1024
PAST_LEN = 512

GRID_W = 64
N_EVEN = (DEPTH + 1) // 2
N_ODD = DEPTH // 2
SSD_HEADS = 16
SSD_HEAD_DIM = 64
SSD_INNER = SSD_HEADS * SSD_HEAD_DIM
SSD_STATE = 128
SSD_GROUPS = 2
SSD_CONV_W = 3
SSD_CONV_CH = SSD_INNER + 2 * SSD_GROUPS * SSD_STATE
SSD_CHUNK = 64
RET_HEADS = 4
RET_QK_DIM = 128
RET_V_DIM = 256
RET_QK = RET_HEADS * RET_QK_DIM
RET_V = RET_HEADS * RET_V_DIM
RET_CHUNK = 64
ROPE_BASE = 10000.0
EVEN_IN = SSD_INNER + SSD_CONV_CH + 2 * SSD_HEADS + 2 * RET_QK + 2 * RET_V
EVEN_MIX = SSD_INNER + RET_V
HGRN_HEADS = 8
HGRN_HEAD_DIM = D_MODEL // HGRN_HEADS
HGRN_WIDTH = HGRN_HEADS * HGRN_HEAD_DIM
HGRN_CHUNK = 16
ODD_IN = 5 * HGRN_WIDTH
N_EXPERTS = 32
TOP_K = 4
D_FF = D_MODEL
SWIGLU_LIMIT = 7.0
SWIGLU_ALPHA = 1.702
NORM_EPS = 1e-6

kernel_name = 'hybrid_ssd_retention_hgrn2_moe_diffusion_step'


def rms_norm(x, w):
    xf = x.astype(jnp.float32)
    return (xf * lax.rsqrt(jnp.mean(xf * xf, axis=-1, keepdims=True) + NORM_EPS)).astype(x.dtype) * w


def split_cols(x, sizes):
    offs = np.cumsum(sizes)[:-1].tolist()
    return jnp.split(x, offs, axis=-1)


def centred_conv(x, w, b):
    pad = SSD_CONV_W // 2
    y = lax.conv_general_dilated(x, w[:, None, :], window_strides=(1,), padding=((pad, pad),),
                                 dimension_numbers=('NWC', 'WIO', 'NWC'), feature_group_count=x.shape[-1])
    return y + b


def axial_rope(length):
    rows = length // GRID_W
    r = jnp.broadcast_to(jnp.arange(rows, dtype=jnp.float32)[:, None], (rows, GRID_W)).reshape(-1)
    col = jnp.broadcast_to(jnp.arange(GRID_W, dtype=jnp.float32)[None, :], (rows, GRID_W)).reshape(-1)
    n_freq = RET_QK_DIM // 4
    inv = ROPE_BASE ** (-jnp.arange(n_freq, dtype=jnp.float32) / n_freq)
    ang = jnp.concatenate([r[:, None] * inv, col[:, None] * inv], axis=-1)
    return jnp.cos(ang), jnp.sin(ang)


def apply_rope(x, cos, sin):
    xf = x.astype(jnp.float32)
    x1, x2 = xf[..., 0::2], xf[..., 1::2]
    c, s = cos[None, :, None, :], sin[None, :, None, :]
    return jnp.stack([x1 * c - x2 * s, x1 * s + x2 * c], axis=-1).reshape(x.shape).astype(x.dtype)


def scalar_decay_scan(q, k, v, log_a, h0, chunk):
    f32 = jnp.float32
    b, l, h, n = q.shape
    pdim = v.shape[-1]
    nc = l // chunk
    qc = q.astype(f32).reshape(b, nc, chunk, h, n)
    kc = k.astype(f32).reshape(b, nc, chunk, h, n)
    vc = v.astype(f32).reshape(b, nc, chunk, h, pdim)
    cum = jnp.cumsum(log_a.astype(f32).reshape(b, nc, chunk, h), axis=2)
    cum_h = jnp.swapaxes(cum, 2, 3)
    causal = jnp.tril(jnp.ones((chunk, chunk), dtype=bool))
    decay = jnp.exp(jnp.where(causal, cum_h[..., :, None] - cum_h[..., None, :], -jnp.inf))
    scores = jnp.einsum('bcthn,bcshn->bchts', qc, kc) * decay
    y = jnp.einsum('bchts,bcshp->bcthp', scores, vc)
    to_end = jnp.exp(cum_h[..., -1:] - cum_h)
    chunk_states = jnp.einsum('bchs,bcshn,bcshp->bchnp', to_end, kc, vc)
    chunk_decay = jnp.exp(cum_h[..., -1])

    def step(state, inp):
        cs, cd = inp
        return state * cd[..., None, None] + cs, state

    final, prev = lax.scan(step, h0.astype(f32), (jnp.moveaxis(chunk_states, 1, 0), jnp.moveaxis(chunk_decay, 1, 0)))
    prev = jnp.moveaxis(prev, 0, 1)
    y = y + jnp.einsum('bcthn,bchnp->bcthp', qc * jnp.exp(cum)[..., None], prev)
    return y.reshape(b, l, h, pdim), final


def vector_decay_scan(q, k, v, log_f, h0, chunk):
    f32 = jnp.float32
    b, l, h, n = q.shape
    pdim = v.shape[-1]
    nc = l // chunk
    qc = q.astype(f32).reshape(b, nc, chunk, h, n)
    kc = k.astype(f32).reshape(b, nc, chunk, h, n)
    vc = v.astype(f32).reshape(b, nc, chunk, h, pdim)
    cum = jnp.cumsum(log_f.astype(f32).reshape(b, nc, chunk, h, n), axis=2)
    causal = jnp.tril(jnp.ones((chunk, chunk), dtype=bool))[:, :, None, None]
    decay = jnp.exp(jnp.where(causal, cum[:, :, :, None] - cum[:, :, None, :], -jnp.inf))
    scores = jnp.einsum('bcthn,bctshn,bcshn->bchts', qc, decay, kc)
    y = jnp.einsum('bchts,bcshp->bcthp', scores, vc)
    k_end = kc * jnp.exp(cum[:, :, -1:] - cum)
    chunk_states = jnp.einsum('bcshn,bcshp->bchnp', k_end, vc)
    chunk_decay = jnp.exp(cum[:, :, -1])

    def step(state, inp):
        cs, cd = inp
        return state * cd[..., None] + cs, state

    final, prev = lax.scan(step, h0.astype(f32), (jnp.moveaxis(chunk_states, 1, 0), jnp.moveaxis(chunk_decay, 1, 0)))
    prev = jnp.moveaxis(prev, 0, 1)
    y = y + jnp.einsum('bcthn,bchnp->bcthp', qc * jnp.exp(cum), prev)
    return y.reshape(b, l, h, pdim), final


def bidir_scan(scan_fn, chunk, fwd_args, bwd_args, h0_fwd, h0_bwd):
    y_f, s_f = scan_fn(*fwd_args, h0_fwd, chunk)
    y_b, s_b = scan_fn(*(jnp.flip(a, axis=1) for a in bwd_args), h0_bwd, chunk)
    return y_f + jnp.flip(y_b, axis=1), s_f, s_b


def even_mixer(h, j, h0_ssd, h0_ret, rope, p):
    f32 = jnp.float32
    b, l, _ = h.shape
    proj = h @ p['even_in_w'][j]
    z, xbc, dt_raw, q, k, v, g = split_cols(proj, [SSD_INNER, SSD_CONV_CH, 2 * SSD_HEADS, RET_QK, RET_QK, RET_V, RET_V])
    xbc = jax.nn.silu(centred_conv(xbc, p['ssd_conv_w'][j], p['ssd_conv_b'][j]))
    xs, bm, cm = split_cols(xbc, [SSD_INNER, SSD_GROUPS * SSD_STATE, SSD_GROUPS * SSD_STATE])
    xs = xs.reshape(b, l, SSD_HEADS, SSD_HEAD_DIM)
    rep = SSD_HEADS // SSD_GROUPS
    bm = jnp.repeat(bm.reshape(b, l, SSD_GROUPS, SSD_STATE), rep, axis=2)
    cm = jnp.repeat(cm.reshape(b, l, SSD_GROUPS, SSD_STATE), rep, axis=2)
    dt = jax.nn.softplus(dt_raw.reshape(b, l, 2, SSD_HEADS).astype(f32) + p['ssd_dt_bias'][j].astype(f32))
    log_a = -dt * jnp.exp(p['ssd_a_log'][j].astype(f32))
    v_dir = xs.astype(f32)[:, :, None] * dt[..., None]
    y, sf, sb = bidir_scan(scalar_decay_scan, SSD_CHUNK,
                           (cm, bm, v_dir[:, :, 0], log_a[:, :, 0]),
                           (cm, bm, v_dir[:, :, 1], log_a[:, :, 1]),
                           h0_ssd[:, 0], h0_ssd[:, 1])
    y = y + xs.astype(f32) * p['ssd_d'][j].astype(f32)[:, None]
    y = y.reshape(b, l, SSD_INNER).astype(h.dtype)
    y_ssd = rms_norm(y * jax.nn.silu(z), p['ssd_norm_w'][j])
    q = q.reshape(b, l, RET_HEADS, RET_QK_DIM)
    k = k.reshape(b, l, RET_HEADS, RET_QK_DIM) * (RET_QK_DIM ** -0.5)
    if rope is not None:
        q = apply_rope(q, rope[0], rope[1])
        k = apply_rope(k, rope[0], rope[1])
    v = v.reshape(b, l, RET_HEADS, RET_V_DIM)
    log_gamma = jax.nn.log_sigmoid(p['ret_decay_logit'][j].astype(f32))
    la = jnp.broadcast_to(log_gamma[:, None, None, :], (2, b, l, RET_HEADS))
    o, rf, rb = bidir_scan(scalar_decay_scan, RET_CHUNK, (q, k, v, la[0]), (q, k, v, la[1]),
                           h0_ret[:, 0], h0_ret[:, 1])
    mu = jnp.mean(o, axis=-1, keepdims=True)
    var = jnp.mean(jnp.square(o - mu), axis=-1, keepdims=True)
    o = ((o - mu) * lax.rsqrt(var + NORM_EPS)).reshape(b, l, RET_V).astype(h.dtype)
    y_ret = o * jax.nn.silu(g)
    out = jnp.concatenate([y_ssd, y_ret], axis=-1) @ p['even_out_w'][j]
    return out, jnp.stack([sf, sb], axis=1), jnp.stack([rf, rb], axis=1)


def odd_mixer(h, j, lb, h0, p):
    f32 = jnp.float32
    b, l, _ = h.shape
    proj = h @ p['odd_in_w'][j]
    q, f_fw, f_bw, i, g = split_cols(proj, [HGRN_WIDTH] * 5)
    shp = (b, l, HGRN_HEADS, HGRN_HEAD_DIM)
    lb = lb.reshape(2, HGRN_HEADS, HGRN_HEAD_DIM)

    def log_forget(fz, lbd):
        return jnp.logaddexp(jnp.log(lbd), jnp.log1p(-lbd) + jax.nn.log_sigmoid(fz.reshape(shp).astype(f32)))

    lf_f = log_forget(f_fw, lb[0])
    lf_b = log_forget(f_bw, lb[1])
    k_f = -jnp.expm1(lf_f)
    k_b = -jnp.expm1(lf_b)
    q = q.reshape(shp)
    i = i.reshape(shp)
    o, sf, sb = bidir_scan(vector_decay_scan, HGRN_CHUNK, (q, k_f, i, lf_f), (q, k_b, i, lf_b),
                           h0[:, 0], h0[:, 1])
    o = o * lax.rsqrt(jnp.mean(o * o, axis=-1, keepdims=True) + NORM_EPS)
    o = o.reshape(b, l, HGRN_WIDTH).astype(h.dtype) * p['hgrn_norm_w'][j]
    out = (o * jax.nn.silu(g)) @ p['odd_out_w'][j]
    return out, jnp.stack([sf, sb], axis=1)


def moe_ffn(h, layer, p):
    b, s, d = h.shape
    t = h.reshape(b * s, d)
    logits = (t @ p['router_w'][layer] + p['router_b'][layer]).astype(jnp.float32)
    top_v, top_i = lax.top_k(logits, TOP_K)
    top_p = jax.nn.softmax(top_v, axis=-1)
    gates = jnp.einsum('tk,tke->te', top_p, jax.nn.one_hot(top_i, N_EXPERTS, dtype=jnp.float32)).astype(t.dtype)
    out = jnp.zeros_like(t)
    for e in range(N_EXPERTS):
        gt = jnp.minimum(t @ p['exp_w_gate'][layer, e] + p['exp_b_gate'][layer, e], SWIGLU_LIMIT)
        up = jnp.clip(t @ p['exp_w_up'][layer, e] + p['exp_b_up'][layer, e], -SWIGLU_LIMIT, SWIGLU_LIMIT)
        act = gt * jax.nn.sigmoid(SWIGLU_ALPHA * gt) * (up + 1)
        out = out + gates[:, e:e + 1] * (act @ p['exp_w_down'][layer, e] + p['exp_b_down'][layer, e])
    return out.reshape(b, s, d)


def run_trunk(x, cond, h0_ssd, h0_ret, h0_hgrn, rope, lb_all, p):
    states_ssd, states_ret, states_hgrn = [], [], []
    sc = jax.nn.silu(cond)
    for layer in range(DEPTH):
        mod = (sc @ p['mod_w'][layer] + p['mod_b'][layer])[:, None, :]
        sh_m, sc_m, g_m, sh_f, sc_f, g_f = jnp.split(mod, 6, axis=-1)
        hmix = rms_norm(x, p['norm_mix_w'][layer]) * (1 + sc_m) + sh_m
        j = layer // 2
        if layer % 2 == 0:
            out, s_ssd, s_ret = even_mixer(hmix, j, h0_ssd[:, j], h0_ret[:, j], rope, p)
            states_ssd.append(s_ssd)
            states_ret.append(s_ret)
        else:
            out, s_h = odd_mixer(hmix, j, lb_all[j], h0_hgrn[:, j], p)
            states_hgrn.append(s_h)
        x = x + g_m * out
        hffn = rms_norm(x, p['norm_ffn_w'][layer]) * (1 + sc_f) + sh_f
        x = x + g_f * moe_ffn(hffn, layer, p)
    return rms_norm(x, p['final_norm_w']), states_ssd, states_ret, states_hgrn


def setup_inputs(seed: int = 0) -> dict:
    key = jax.random.key(seed)
    kit = iter(jax.random.split(key, 40))
    f32 = jnp.float32

    def nrm(shape, std):
        return jax.random.normal(next(kit), shape, f32) * std

    x_prompt = nrm((BATCH, SEQ, D_MODEL), 1.0)
    x_sample = nrm((DEC_BATCH, DEC_SEQ, D_MODEL), 1.0)
    state_ssd = nrm((DEC_BATCH, N_EVEN, 2, SSD_HEADS, SSD_STATE, SSD_HEAD_DIM), 0.5)
    state_ret = nrm((DEC_BATCH, N_EVEN, 2, RET_HEADS, RET_QK_DIM, RET_V_DIM), 0.5)
    state_hgrn = nrm((DEC_BATCH, N_ODD, 2, HGRN_HEADS, HGRN_HEAD_DIM, HGRN_HEAD_DIM), 0.5)
    c = nrm((DEC_BATCH, D_MODEL), 1.0)
    c_ctx = nrm((D_MODEL,), 1.0)
    mod_w = nrm((DEPTH, D_MODEL, 6 * D_MODEL), 0.5 * D_MODEL ** -0.5)
    mod_b = nrm((DEPTH, 6 * D_MODEL), 0.02)
    norm_mix_w = 1.0 + nrm((DEPTH, D_MODEL), 0.02)
    norm_ffn_w = 1.0 + nrm((DEPTH, D_MODEL), 0.02)
    even_in_w = nrm((N_EVEN, D_MODEL, EVEN_IN), D_MODEL ** -0.5)
    ssd_conv_w = nrm((N_EVEN, SSD_CONV_W, SSD_CONV_CH), SSD_CONV_W ** -0.5)
    ssd_conv_b = nrm((N_EVEN, SSD_CONV_CH), 0.02)
    dt0 = jnp.exp(jax.random.uniform(next(kit), (N_EVEN, 2, SSD_HEADS), f32, math.log(1e-3), math.log(1e-1)))
    ssd_dt_bias = dt0 + jnp.log(-jnp.expm1(-dt0))
    ssd_a_log = jnp.log(jax.random.uniform(next(kit), (N_EVEN, 2, SSD_HEADS), f32, 1.0, 16.0))
    ssd_d = 1.0 + nrm((N_EVEN, SSD_HEADS), 0.1)
    ssd_norm_w = 1.0 + nrm((N_EVEN, SSD_INNER), 0.02)
    ret_decay_logit = jnp.log(2.0 ** (5.0 + jnp.arange(RET_HEADS, dtype=f32)) - 1.0) + nrm((N_EVEN, 2, RET_HEADS), 0.05)
    even_out_w = nrm((N_EVEN, EVEN_MIX, D_MODEL), EVEN_MIX ** -0.5)
    odd_in_w = nrm((N_ODD, D_MODEL, ODD_IN), D_MODEL ** -0.5)
    hgrn_lower_bound = nrm((N_ODD, 2, HGRN_WIDTH), 1.0)
    hgrn_norm_w = 1.0 + nrm((N_ODD, HGRN_WIDTH), 0.02)
    odd_out_w = nrm((N_ODD, HGRN_WIDTH, D_MODEL), HGRN_WIDTH ** -0.5)
    router_w = nrm((DEPTH, D_MODEL, N_EXPERTS), D_MODEL ** -0.5)
    router_b = nrm((DEPTH, N_EXPERTS), 0.01)
    exp_w_gate = nrm((DEPTH, N_EXPERTS, D_MODEL, D_FF), D_MODEL ** -0.5)
    exp_b_gate = nrm((DEPTH, N_EXPERTS, D_FF), 0.02)
    exp_w_up = nrm((DEPTH, N_EXPERTS, D_MODEL, D_FF), D_MODEL ** -0.5)
    exp_b_up = nrm((DEPTH, N_EXPERTS, D_FF), 0.02)
    exp_w_down = nrm((DEPTH, N_EXPERTS, D_FF, D_MODEL), D_FF ** -0.5)
    exp_b_down = nrm((DEPTH, N_EXPERTS, D_MODEL), 0.02)
    final_norm_w = 1.0 + nrm((D_MODEL,), 0.02)
    return {'x_prompt': x_prompt, 'x_sample': x_sample, 'state_ssd': state_ssd, 'state_ret': state_ret,
            'state_hgrn': state_hgrn, 'c': c, 'c_ctx': c_ctx, 'mod_w': mod_w, 'mod_b': mod_b,
            'norm_mix_w': norm_mix_w, 'norm_ffn_w': norm_ffn_w, 'even_in_w': even_in_w,
            'ssd_conv_w': ssd_conv_w, 'ssd_conv_b': ssd_conv_b, 'ssd_dt_bias': ssd_dt_bias,
            'ssd_a_log': ssd_a_log, 'ssd_d': ssd_d, 'ssd_norm_w': ssd_norm_w,
            'ret_decay_logit': ret_decay_logit, 'even_out_w': even_out_w, 'odd_in_w': odd_in_w,
            'hgrn_lower_bound': hgrn_lower_bound, 'hgrn_norm_w': hgrn_norm_w, 'odd_out_w': odd_out_w,
            'router_w': router_w, 'router_b': router_b, 'exp_w_gate': exp_w_gate, 'exp_b_gate': exp_b_gate,
            'exp_w_up': exp_w_up, 'exp_b_up': exp_b_up, 'exp_w_down': exp_w_down, 'exp_b_down': exp_b_down,
            'final_norm_w': final_norm_w}


def reference(x_prompt, x_sample, state_ssd, state_ret, state_hgrn, c, c_ctx, mod_w, mod_b, norm_mix_w,
              norm_ffn_w, even_in_w, ssd_conv_w, ssd_conv_b, ssd_dt_bias, ssd_a_log, ssd_d, ssd_norm_w,
              ret_decay_logit, even_out_w, odd_in_w, hgrn_lower_bound, hgrn_norm_w, odd_out_w, router_w,
              router_b, exp_w_gate, exp_b_gate, exp_w_up, exp_b_up, exp_w_down, exp_b_down, final_norm_w):
    p = dict(mod_w=mod_w, mod_b=mod_b, norm_mix_w=norm_mix_w, norm_ffn_w=norm_ffn_w, even_in_w=even_in_w,
             ssd_conv_w=ssd_conv_w, ssd_conv_b=ssd_conv_b, ssd_dt_bias=ssd_dt_bias, ssd_a_log=ssd_a_log,
             ssd_d=ssd_d, ssd_norm_w=ssd_norm_w, ret_decay_logit=ret_decay_logit, even_out_w=even_out_w,
             odd_in_w=odd_in_w, hgrn_norm_w=hgrn_norm_w, odd_out_w=odd_out_w, router_w=router_w,
             router_b=router_b, exp_w_gate=exp_w_gate, exp_b_gate=exp_b_gate, exp_w_up=exp_w_up,
             exp_b_up=exp_b_up, exp_w_down=exp_w_down, exp_b_down=exp_b_down, final_norm_w=final_norm_w)
    lb_all = jnp.cumsum(jax.nn.softmax(hgrn_lower_bound.astype(jnp.float32), axis=0), axis=0)
    lb_all = lb_all - lb_all[0]

    bp = x_prompt.shape[0]
    z_ssd = jnp.zeros((bp, N_EVEN, 2, SSD_HEADS, SSD_STATE, SSD_HEAD_DIM), x_prompt.dtype)
    z_ret = jnp.zeros((bp, N_EVEN, 2, RET_HEADS, RET_QK_DIM, RET_V_DIM), x_prompt.dtype)
    z_hgrn = jnp.zeros((bp, N_ODD, 2, HGRN_HEADS, HGRN_HEAD_DIM, HGRN_HEAD_DIM), x_prompt.dtype)
    y_prompt, ctx_ssd, ctx_ret, ctx_hgrn = run_trunk(x_prompt, c_ctx[None, :], z_ssd, z_ret, z_hgrn, None, lb_all, p)
    new_state_ssd = jnp.stack(ctx_ssd, axis=1).astype(x_prompt.dtype)
    new_state_ret = jnp.stack(ctx_ret, axis=1).astype(x_prompt.dtype)
    new_state_hgrn = jnp.stack(ctx_hgrn, axis=1).astype(x_prompt.dtype)

    rope = axial_rope(x_sample.shape[1])
    y_sample, _, _, _ = run_trunk(x_sample, c, state_ssd, state_ret, state_hgrn, rope, lb_all, p)
    return (y_prompt, y_sample, new_state_ssd, new_state_ret, new_state_hgrn)
```

```python
import functools
import math

import jax
import jax.numpy as jnp
from jax import lax
from jax.experimental import pallas as pl
from jax.experimental.pallas import tpu as pltpu

F32 = jnp.float32
BF16 = jnp.bfloat16
HIGHEST = lax.Precision.HIGHEST

D_MODEL = 1024
GRID_W = 64
SSD_HEADS = 16
SSD_HEAD_DIM = 64
SSD_INNER = SSD_HEADS * SSD_HEAD_DIM
SSD_STATE = 128
SSD_GROUPS = 2
SSD_CONV_CH = SSD_INNER + 2 * SSD_GROUPS * SSD_STATE
RET_HEADS = 4
RET_QK_DIM = 128
RET_V_DIM = 256
RET_QK = RET_HEADS * RET_QK_DIM
RET_V = RET_HEADS * RET_V_DIM
ROPE_BASE = 10000.0
HGRN_HEADS = 8
HGRN_HEAD_DIM = 128
N_EXPERTS = 32
TOP_K = 4
SWIGLU_LIMIT = 7.0
SWIGLU_ALPHA = 1.702
NORM_EPS = 1e-6

LANES = 128
TOKEN_BLOCK = 256
SCAN_CHUNK = 256
EXPERT_TILE = 256
COND_ROWS = 16
NEG_BIG = -1e30
VMEM_LIMIT = 56 * 1024 * 1024


def _cparams(n_axes):
    return pltpu.CompilerParams(dimension_semantics=("arbitrary",) * n_axes,
                                vmem_limit_bytes=VMEM_LIMIT)


def _sigmoid(x):
    return 1.0 / (1.0 + jnp.exp(-x))


def _silu(x):
    return x * _sigmoid(x)


def _softplus(x):
    return jnp.maximum(x, 0.0) + jnp.log(1.0 + jnp.exp(-jnp.abs(x)))


def _log_sigmoid(x):
    return jnp.minimum(x, 0.0) - jnp.log(1.0 + jnp.exp(-jnp.abs(x)))


def _dot(a, b):
    return jnp.dot(a, b, preferred_element_type=F32)


def _dot_nt(a, b):
    return lax.dot_general(a, b, (((1,), (1,)), ((), ())), preferred_element_type=F32)


def _dot_tn(a, b):
    return lax.dot_general(a, b, (((0,), (0,)), ((), ())), preferred_element_type=F32)


def _cond_row(i, prompt_blocks, blocks_per_request):
    return jnp.where(i < prompt_blocks, 0, 1 + (i - prompt_blocks) // blocks_per_request)


def _mod_kernel(cond_ref, w_ref, b_ref, o_ref):
    sc = _silu(cond_ref[...]).astype(BF16)
    o_ref[...] = _dot(sc, w_ref[...].astype(BF16)) + b_ref[...]


def _modulation(cond, mod_w, mod_b):
    depth, d, n = mod_w.shape
    tn = 1536
    return pl.pallas_call(
        _mod_kernel,
        grid=(depth, n // tn),
        in_specs=[pl.BlockSpec((COND_ROWS, d), lambda l, j: (0, 0)),
                  pl.BlockSpec((None, d, tn), lambda l, j: (l, 0, j)),
                  pl.BlockSpec((None, 1, tn), lambda l, j: (l, 0, j))],
        out_specs=pl.BlockSpec((None, COND_ROWS, tn), lambda l, j: (l, 0, j)),
        out_shape=jax.ShapeDtypeStruct((depth, COND_ROWS, n), F32),
        compiler_params=_cparams(2),
        name="modulation",
    )(cond, mod_w, mod_b.reshape(depth, 1, n))


def _norm_mod(x, nw, mod, shift_idx, scale_idx):
    d = x.shape[-1]
    xn = x * lax.rsqrt(jnp.mean(x * x, axis=-1, keepdims=True) + NORM_EPS) * nw
    sh = mod[:, shift_idx * d:(shift_idx + 1) * d]
    sc = mod[:, scale_idx * d:(scale_idx + 1) * d]
    return xn * (1.0 + sc) + sh


def _inproj_kernel(x_ref, mod_ref, nw_ref, w_ref, *out_refs, splits):
    h = _norm_mod(x_ref[...], nw_ref[...], mod_ref[...], 0, 1).astype(BF16)
    off = 0
    for o_ref, n in zip(out_refs, splits):
        o_ref[...] = _dot(h, w_ref[:, off:off + n]).astype(o_ref.dtype)
        off += n


def _inproj(x, mod_l, nw, w, splits, dtypes, cond_map):
    t, d = x.shape
    n = w.shape[1]
    assert sum(splits) == n
    nb = t // TOKEN_BLOCK
    return pl.pallas_call(
        functools.partial(_inproj_kernel, splits=tuple(splits)),
        grid=(nb,),
        in_specs=[pl.BlockSpec((TOKEN_BLOCK, d), lambda i: (i, 0)),
                  pl.BlockSpec((None, 1, mod_l.shape[-1]), lambda i: (cond_map(i), 0, 0)),
                  pl.BlockSpec((1, d), lambda i: (0, 0)),
                  pl.BlockSpec((d, n), lambda i: (0, 0))],
        out_specs=[pl.BlockSpec((TOKEN_BLOCK, s), lambda i: (i, 0)) for s in splits],
        out_shape=[jax.ShapeDtypeStruct((t, s), dt) for s, dt in zip(splits, dtypes)],
        compiler_params=_cparams(1),
        name="inproj",
    )(x, mod_l, nw, w)


def _even_out_kernel(yssd_ref, z_ref, nw_ref, yret_ref, w_ref, x_ref, mod_ref, o_ref):
    d = x_ref.shape[-1]
    u = yssd_ref[...] * _silu(z_ref[...])
    un = u * lax.rsqrt(jnp.mean(u * u, axis=-1, keepdims=True) + NORM_EPS) * nw_ref[...]
    k1 = yssd_ref.shape[-1]
    acc = _dot(un.astype(BF16), w_ref[0:k1, :]) + _dot(yret_ref[...], w_ref[k1:, :])
    o_ref[...] = x_ref[...] + mod_ref[:, 2 * d:3 * d] * acc


def _even_out(yssd, z, nw, yret, w, x, mod_l, cond_map):
    t, d = x.shape
    nb = t // TOKEN_BLOCK
    row = lambda i: (i, 0)
    const = lambda i: (0, 0)
    return pl.pallas_call(
        _even_out_kernel,
        grid=(nb,),
        in_specs=[pl.BlockSpec((TOKEN_BLOCK, yssd.shape[1]), row),
                  pl.BlockSpec((TOKEN_BLOCK, z.shape[1]), row),
                  pl.BlockSpec((1, nw.shape[1]), const),
                  pl.BlockSpec((TOKEN_BLOCK, yret.shape[1]), row),
                  pl.BlockSpec(w.shape, const),
                  pl.BlockSpec((TOKEN_BLOCK, d), row),
                  pl.BlockSpec((None, 1, mod_l.shape[-1]), lambda i: (cond_map(i), 0, 0))],
        out_specs=pl.BlockSpec((TOKEN_BLOCK, d), row),
        out_shape=jax.ShapeDtypeStruct((t, d), F32),
        compiler_params=_cparams(1),
        name="even_out",
    )(yssd, z, nw, yret, w, x, mod_l)


def _odd_out_kernel(a_ref, w_ref, x_ref, mod_ref, o_ref):
    d = x_ref.shape[-1]
    o_ref[...] = x_ref[...] + mod_ref[:, 2 * d:3 * d] * _dot(a_ref[...], w_ref[...])


def _odd_out(a, w, x, mod_l, cond_map):
    t, d = x.shape
    nb = t // TOKEN_BLOCK
    row = lambda i: (i, 0)
    return pl.pallas_call(
        _odd_out_kernel,
        grid=(nb,),
        in_specs=[pl.BlockSpec((TOKEN_BLOCK, a.shape[1]), row),
                  pl.BlockSpec(w.shape, lambda i: (0, 0)),
                  pl.BlockSpec((TOKEN_BLOCK, d), row),
                  pl.BlockSpec((None, 1, mod_l.shape[-1]), lambda i: (cond_map(i), 0, 0))],
        out_specs=pl.BlockSpec((TOKEN_BLOCK, d), row),
        out_shape=jax.ShapeDtypeStruct((t, d), F32),
        compiler_params=_cparams(1),
        name="odd_out",
    )(a, w, x, mod_l)


def _tri(q, upper):
    r = lax.broadcasted_iota(jnp.int32, (q, q), 0)
    c = lax.broadcasted_iota(jnp.int32, (q, q), 1)
    return (c >= r) if upper else (c <= r)


def _cumsum_rows(x, suffix):
    q = x.shape[0]
    m = _tri(q, upper=suffix).astype(F32)
    return jnp.dot(m, x, precision=HIGHEST, preferred_element_type=F32)


def _ssd_kernel(*refs, seq_len, has_h0, emit_state):
    it = iter(refs)
    x_ref, b_ref, c_ref = next(it), next(it), next(it)
    cwx_ref, cwb_ref, cwc_ref = next(it), next(it), next(it)
    cbx_ref, cbb_ref, cbc_ref = next(it), next(it), next(it)
    dt_ref, dtb_ref, alog_ref, dskip_ref = next(it), next(it), next(it), next(it)
    h0_ref = next(it) if has_h0 else None
    next(it)
    y_ref = next(it)
    st_ref = next(it) if emit_state else None

    L = seq_len
    q = SCAN_CHUNK
    nc = L // q
    pair = pl.program_id(1)
    half = SSD_HEAD_DIM

    row = lax.broadcasted_iota(jnp.int32, (L, LANES), 0)

    def conv_silu(ref, w_ref, bias_ref):
        x = ref[...]
        w = w_ref[...]
        prev = jnp.where(row == 0, 0.0, pltpu.roll(x, 1, axis=0))
        nxt = jnp.where(row == L - 1, 0.0, pltpu.roll(x, L - 1, axis=0))
        return _silu(prev * w[0:1, :] + x * w[1:2, :] + nxt * w[2:3, :] + bias_ref[...])

    xs = conv_silu(x_ref, cwx_ref, cbx_ref)
    bm = conv_silu(b_ref, cwb_ref, cbb_ref)
    cm = conv_silu(c_ref, cwc_ref, cbc_ref)
    xs_bf, bm_bf, cm_bf = xs.astype(BF16), bm.astype(BF16), cm.astype(BF16)

    dt = _softplus(dt_ref[...] + dtb_ref[...])
    la = -dt * jnp.exp(alog_ref[...])

    lane = lax.broadcasted_iota(jnp.int32, (q, LANES), 1)
    lane_row = lax.broadcasted_iota(jnp.int32, (1, LANES), 1)
    first = lane < half
    tril = _tri(q, upper=False)
    triu = _tri(q, upper=True)

    def pick_col(arr, col):
        return jnp.sum(jnp.where(lane == col, arr, 0.0), axis=1, keepdims=True)

    def pick_row(arr_t, r):
        rr = lax.broadcasted_iota(jnp.int32, arr_t.shape, 0)
        return jnp.sum(jnp.where(rr == r, arr_t, 0.0), axis=0, keepdims=True)

    y_parts, e_f, e_b, d_sf, d_sb, dec_f, dec_b = [], [], [], [], [], [], []
    for c in range(nc):
        sl = slice(c * q, (c + 1) * q)
        la_c, dt_c = la[sl], dt[sl]
        cum = _cumsum_rows(la_c, suffix=False)
        rcs = _cumsum_rows(la_c, suffix=True)
        cum_t, rcs_t, dt_t = cum.T, rcs.T, dt_c.T
        s_g = _dot_nt(cm_bf[sl], bm_bf[sl])
        y_c = None
        cols = {}
        for hh in range(2):
            head = 2 * pair + hh
            cf, cb = pick_col(cum, head), pick_col(rcs, SSD_HEADS + head)
            rf, rb = pick_row(cum_t, head), pick_row(rcs_t, SSD_HEADS + head)
            dtf_r, dtb_r = pick_row(dt_t, head), pick_row(dt_t, SSD_HEADS + head)
            dec = (jnp.exp(jnp.where(tril, cf - rf, NEG_BIG)) * dtf_r
                   + jnp.exp(jnp.where(triu, cb - rb, NEG_BIG)) * dtb_r)
            y_h = _dot((s_g * dec).astype(BF16), xs_bf[sl])
            y_c = y_h if hh == 0 else jnp.where(first, y_c, y_h)
            cols[hh] = (cf, cb, pick_col(dt_c, head), pick_col(dt_c, SSD_HEADS + head))
        cum_e = jnp.where(first, cols[0][0], cols[1][0])
        rcs_e = jnp.where(first, cols[0][1], cols[1][1])
        dtf_e = jnp.where(first, cols[0][2], cols[1][2])
        dtb_e = jnp.where(first, cols[0][3], cols[1][3])
        y_parts.append(y_c)
        e_f.append(jnp.exp(cum_e))
        e_b.append(jnp.exp(rcs_e))
        xs_c = xs[sl]
        wf = xs_c * dtf_e * jnp.exp(cum_e[q - 1:q, :] - cum_e)
        wb = xs_c * dtb_e * jnp.exp(rcs_e[0:1, :] - rcs_e)
        d_sf.append(_dot_tn(bm_bf[sl], wf.astype(BF16)))
        d_sb.append(_dot_tn(bm_bf[sl], wb.astype(BF16)))
        dec_f.append(jnp.exp(cum_e[q - 1:q, :]))
        dec_b.append(jnp.exp(rcs_e[0:1, :]))

    zero = jnp.zeros((SSD_STATE, LANES), F32)
    sf = h0_ref[0] if has_h0 else zero
    sf_in = []
    for c in range(nc):
        sf_in.append(sf)
        sf = dec_f[c] * sf + d_sf[c]
    sb = h0_ref[1] if has_h0 else zero
    sb_in = [None] * nc
    for c in reversed(range(nc)):
        sb_in[c] = sb
        sb = dec_b[c] * sb + d_sb[c]

    dsk = dskip_ref[...]
    for c in range(nc):
        sl = slice(c * q, (c + 1) * q)
        y_c = y_parts[c] + xs[sl] * dsk
        if has_h0 or c > 0:
            y_c = y_c + e_f[c] * _dot(cm_bf[sl], sf_in[c].astype(BF16))
        if has_h0 or c < nc - 1:
            y_c = y_c + e_b[c] * _dot(cm_bf[sl], sb_in[c].astype(BF16))
        y_ref[sl, :] = y_c
    if emit_state:
        st_ref[0] = sf
        st_ref[1] = sb
    del lane_row


def _ssd_scan(xbc, dt_raw, conv_w, conv_b, dt_bias, a_log, d_skip, *, row0, n_seq, seq_len,
              h0, prev_out):
    t = xbc.shape[0]
    blk0 = row0 // seq_len
    n_pairs = SSD_HEADS // 2
    pairs_per_group = n_pairs // SSD_GROUPS
    xoff, boff, coff = 0, SSD_INNER // LANES, (SSD_INNER + SSD_GROUPS * SSD_STATE) // LANES
    has_h0 = h0 is not None
    emit_state = not has_h0

    def seq_blk(col_fn):
        return lambda s, p: (blk0 + s, col_fn(p))

    fx = lambda p: xoff + p
    fb = lambda p: boff + p // pairs_per_group
    fc = lambda p: coff + p // pairs_per_group
    vec = lambda rows, col_fn: pl.BlockSpec((rows, LANES), lambda s, p: (0, col_fn(p)))
    in_specs = [pl.BlockSpec((seq_len, LANES), seq_blk(fx)),
                pl.BlockSpec((seq_len, LANES), seq_blk(fb)),
                pl.BlockSpec((seq_len, LANES), seq_blk(fc)),
                vec(3, fx), vec(3, fb), vec(3, fc),
                vec(1, fx), vec(1, fb), vec(1, fc),
                pl.BlockSpec((seq_len, LANES), lambda s, p: (blk0 + s, 0)),
                pl.BlockSpec((1, LANES), lambda s, p: (0, 0)),
                pl.BlockSpec((1, LANES), lambda s, p: (0, 0)),
                vec(1, fx)]
    args = [xbc, xbc, xbc, conv_w, conv_w, conv_w, conv_b, conv_b, conv_b,
            dt_raw, dt_bias, a_log, d_skip]
    io_alias = {}
    if has_h0:
        in_specs.append(pl.BlockSpec((None, 2, None, SSD_STATE, LANES), lambda s, p: (s, 0, p, 0, 0)))
        args.append(h0)
    io_alias = {len(args): 0}
    in_specs.append(pl.BlockSpec(memory_space=pl.ANY))
    args.append(prev_out)
    out_specs = [pl.BlockSpec((seq_len, LANES), lambda s, p: (blk0 + s, p))]
    out_shape = [jax.ShapeDtypeStruct((t, SSD_INNER), F32)]
    if emit_state:
        out_specs.append(pl.BlockSpec((None, 2, None, SSD_STATE, LANES), lambda s, p: (s, 0, p, 0, 0)))
        out_shape.append(jax.ShapeDtypeStruct((n_seq, 2, n_pairs, SSD_STATE, LANES), F32))
    kern = functools.partial(_ssd_kernel, seq_len=seq_len, has_h0=has_h0, emit_state=emit_state)
    return pl.pallas_call(
        kern, grid=(n_seq, n_pairs), in_specs=in_specs, out_specs=out_specs, out_shape=out_shape,
        input_output_aliases=io_alias, compiler_params=_cparams(2), name="ssd_scan",
    )(*args)


def _ret_kernel(*refs, seq_len, has_h0, emit_state, use_rope):
    it = iter(refs)
    q_ref, k_ref, v_ref, g_ref, lg_ref = next(it), next(it), next(it), next(it), next(it)
    cos_ref = next(it) if use_rope else None
    sin_ref = next(it) if use_rope else None
    h0_ref = next(it) if has_h0 else None
    next(it)
    o_ref = next(it)
    st_ref = next(it) if emit_state else None

    L = seq_len
    qn = SCAN_CHUNK
    nc = L // qn
    head = pl.program_id(1)

    qf = q_ref[...]
    kf = k_ref[...] * (RET_QK_DIM ** -0.5)
    if use_rope:
        lane = lax.broadcasted_iota(jnp.int32, (L, LANES), 1)
        even = (lane % 2) == 0
        cos, sin = cos_ref[...], sin_ref[...]

        def rope(x):
            swapped = jnp.where(even, pltpu.roll(x, LANES - 1, axis=1), pltpu.roll(x, 1, axis=1))
            return x * cos + swapped * sin
        qf, kf = rope(qf), rope(kf)
    q_bf, k_bf = qf.astype(BF16), kf.astype(BF16)
    v_bf = v_ref[...]

    lg = _log_sigmoid(lg_ref[...])
    rr = lax.broadcasted_iota(jnp.int32, lg.shape, 0)
    lgf = jnp.sum(jnp.where(rr == head, lg, 0.0), axis=0, keepdims=True)[:, 0:1]
    lgb = jnp.sum(jnp.where(rr == RET_HEADS + head, lg, 0.0), axis=0, keepdims=True)[:, 0:1]

    ti = lax.broadcasted_iota(jnp.int32, (qn, qn), 0)
    si = lax.broadcasted_iota(jnp.int32, (qn, qn), 1)
    dist = (ti - si).astype(F32)
    dec = (jnp.exp(jnp.where(ti >= si, dist * lgf, NEG_BIG))
           + jnp.exp(jnp.where(si >= ti, -dist * lgb, NEG_BIG)))
    tcol = lax.broadcasted_iota(jnp.int32, (qn, 1), 0).astype(F32)
    ef = jnp.exp((tcol + 1.0) * lgf)
    eb = jnp.exp((qn - tcol) * lgb)
    wf = jnp.exp((qn - 1.0 - tcol) * lgf)
    wb = jnp.exp(tcol * lgb)
    dec_f = jnp.exp(qn * lgf)
    dec_b = jnp.exp(qn * lgb)

    y_parts, d_sf, d_sb = [], [], []
    for c in range(nc):
        sl = slice(c * qn, (c + 1) * qn)
        s = _dot_nt(q_bf[sl], k_bf[sl])
        y_parts.append(_dot((s * dec).astype(BF16), v_bf[sl]))
        d_sf.append(_dot_tn((kf[sl] * wf).astype(BF16), v_bf[sl]))
        d_sb.append(_dot_tn((kf[sl] * wb).astype(BF16), v_bf[sl]))

    zero = jnp.zeros((RET_QK_DIM, RET_V_DIM), F32)
    sf = h0_ref[0] if has_h0 else zero
    sf_in = []
    for c in range(nc):
        sf_in.append(sf)
        sf = dec_f * sf + d_sf[c]
    sb = h0_ref[1] if has_h0 else zero
    sb_in = [None] * nc
    for c in reversed(range(nc)):
        sb_in[c] = sb
        sb = dec_b * sb + d_sb[c]

    for c in range(nc):
        sl = slice(c * qn, (c + 1) * qn)
        o = y_parts[c]
        if has_h0 or c > 0:
            o = o + ef * _dot(q_bf[sl], sf_in[c].astype(BF16))
        if has_h0 or c < nc - 1:
            o = o + eb * _dot(q_bf[sl], sb_in[c].astype(BF16))
        mu = jnp.mean(o, axis=-1, keepdims=True)
        var = jnp.mean(jnp.square(o - mu), axis=-1, keepdims=True)
        on = (o - mu) * lax.rsqrt(var + NORM_EPS)
        o_ref[sl, :] = (on * _silu(g_ref[sl, :])).astype(o_ref.dtype)
    if emit_state:
        st_ref[0] = sf
        st_ref[1] = sb


def _ret_scan(q, k, v, g, lg_rows, rope, *, row0, n_seq, seq_len, h0, prev_out, layer_j):
    t = q.shape[0]
    blk0 = row0 // seq_len
    has_h0 = h0 is not None
    emit_state = not has_h0
    use_rope = rope is not None
    blk = lambda w: pl.BlockSpec((seq_len, w), lambda s, h: (blk0 + s, h))
    in_specs = [blk(RET_QK_DIM), blk(RET_QK_DIM), blk(RET_V_DIM), blk(RET_V_DIM),
                pl.BlockSpec(lg_rows.shape, lambda s, h: (0, 0))]
    args = [q, k, v, g, lg_rows]
    if use_rope:
        in_specs += [pl.BlockSpec((seq_len, LANES), lambda s, h: (0, 0))] * 2
        args += [rope[0], rope[1]]
    if has_h0:
        in_specs.append(pl.BlockSpec((None, None, 2, None, RET_QK_DIM, RET_V_DIM),
                                     lambda s, h: (s, layer_j, 0, h, 0, 0)))
        args.append(h0)
    io_alias = {}
    io_alias = {len(args): 0}
    in_specs.append(pl.BlockSpec(memory_space=pl.ANY))
    args.append(prev_out)
    out_specs = [pl.BlockSpec((seq_len, RET_V_DIM), lambda s, h: (blk0 + s, h))]
    out_shape = [jax.ShapeDtypeStruct((t, RET_V), BF16)]
    if emit_state:
        out_specs.append(pl.BlockSpec((None, 2, None, RET_QK_DIM, RET_V_DIM), lambda s, h: (s, 0, h, 0, 0)))
        out_shape.append(jax.ShapeDtypeStruct((n_seq, 2, RET_HEADS, RET_QK_DIM, RET_V_DIM), F32))
    kern = functools.partial(_ret_kernel, seq_len=seq_len, has_h0=has_h0, emit_state=emit_state,
                             use_rope=use_rope)
    return pl.pallas_call(
        kern, grid=(n_seq, RET_HEADS), in_specs=in_specs, out_specs=out_specs, out_shape=out_shape,
        input_output_aliases=io_alias, compiler_params=_cparams(2), name="ret_scan",
    )(*args)


def _pair_ref(arr, h, off):
    qn, n = arr.shape
    w = 2 * h
    if w >= 8:
        a3 = arr.reshape(qn // w, w, n)
        return jnp.broadcast_to(a3[:, off:off + 1, :], (qn // w, w, n)).reshape(qn, n)
    t = lax.broadcasted_iota(jnp.int32, arr.shape, 0)
    out = jnp.zeros_like(arr)
    for r in range(w):
        out = jnp.where((t % w) == r, pltpu.roll(arr, (r - off) % qn, axis=0), out)
    return out


def _hgrn_kernel(*refs, seq_len, has_h0, emit_state):
    it = iter(refs)
    q_ref, ff_ref, fb_ref, i_ref, g_ref = next(it), next(it), next(it), next(it), next(it)
    llb_ref, l1m_ref, nw_ref = next(it), next(it), next(it)
    h0_ref = next(it) if has_h0 else None
    next(it)
    o_ref = next(it)
    st_ref = next(it) if emit_state else None

    L = seq_len
    qn = SCAN_CHUNK
    nc = L // qn

    def log_forget(fz, d):
        a = llb_ref[d:d + 1, :]
        b = l1m_ref[d:d + 1, :] + _log_sigmoid(fz)
        m = jnp.maximum(a, b)
        return m + jnp.log(1.0 + jnp.exp(-jnp.abs(a - b)))

    lf_f = log_forget(ff_ref[...], 0)
    lf_b = log_forget(fb_ref[...], 1)
    k_f = 1.0 - jnp.exp(lf_f)
    k_b = 1.0 - jnp.exp(lf_b)
    qv = q_ref[...]
    v_bf = i_ref[...]

    t_col = lax.broadcasted_iota(jnp.int32, (qn, 1), 0)
    ti = lax.broadcasted_iota(jnp.int32, (qn, qn), 0)
    si = lax.broadcasted_iota(jnp.int32, (qn, qn), 1)

    y_parts, e_f, e_b, d_sf, d_sb, dec_f, dec_b = [], [], [], [], [], [], []
    for c in range(nc):
        sl = slice(c * qn, (c + 1) * qn)
        q_c, kf_c, kb_c = qv[sl], k_f[sl], k_b[sl]
        cum = _cumsum_rows(lf_f[sl], suffix=False)
        rcs = _cumsum_rows(lf_b[sl], suffix=True)
        scores = jnp.zeros((qn, qn), F32)
        h = 1
        while h < qn:
            upper = ((t_col // h) % 2) == 1
            ref_f = _pair_ref(cum, h, h - 1)
            ref_b = _pair_ref(rcs, h, h)
            aq = jnp.exp(jnp.where(upper, cum - ref_f, rcs - ref_b))
            ak = jnp.exp(jnp.where(upper, ref_b - rcs, ref_f - cum))
            qt = (q_c * aq).astype(BF16)
            kt = (jnp.where(upper, kb_c, kf_c) * ak).astype(BF16)
            s_l = _dot_nt(qt, kt)
            mask = ((ti // (2 * h)) == (si // (2 * h))) & (((ti // h) % 2) != ((si // h) % 2))
            scores = jnp.where(mask, s_l, scores)
            h *= 2
        diag = jnp.sum(q_c * (kf_c + kb_c), axis=-1, keepdims=True)
        y_parts.append(_dot(scores.astype(BF16), v_bf[sl]) + diag * v_bf[sl].astype(F32))
        e_f.append(jnp.exp(cum))
        e_b.append(jnp.exp(rcs))
        d_sf.append(_dot_tn((kf_c * jnp.exp(cum[qn - 1:qn, :] - cum)).astype(BF16), v_bf[sl]))
        d_sb.append(_dot_tn((kb_c * jnp.exp(rcs[0:1, :] - rcs)).astype(BF16), v_bf[sl]))
        n = HGRN_HEAD_DIM
        dec_f.append(jnp.broadcast_to(jnp.exp(cum[qn - 1:qn, :]), (n, n)).T)
        dec_b.append(jnp.broadcast_to(jnp.exp(rcs[0:1, :]), (n, n)).T)

    zero = jnp.zeros((HGRN_HEAD_DIM, HGRN_HEAD_DIM), F32)
    sf = h0_ref[0] if has_h0 else zero
    sf_in = []
    for c in range(nc):
        sf_in.append(sf)
        sf = dec_f[c] * sf + d_sf[c]
    sb = h0_ref[1] if has_h0 else zero
    sb_in = [None] * nc
    for c in reversed(range(nc)):
        sb_in[c] = sb
        sb = dec_b[c] * sb + d_sb[c]

    nw = nw_ref[...]
    for c in range(nc):
        sl = slice(c * qn, (c + 1) * qn)
        o = y_parts[c]
        if has_h0 or c > 0:
            o = o + _dot((qv[sl] * e_f[c]).astype(BF16), sf_in[c].astype(BF16))
        if has_h0 or c < nc - 1:
            o = o + _dot((qv[sl] * e_b[c]).astype(BF16), sb_in[c].astype(BF16))
        on = o * lax.rsqrt(jnp.mean(o * o, axis=-1, keepdims=True) + NORM_EPS) * nw
        o_ref[sl, :] = (on * _silu(g_ref[sl, :])).astype(o_ref.dtype)
    if emit_state:
        st_ref[0] = sf
        st_ref[1] = sb


def _hgrn_scan(q, ff, fb, iv, g, log_lb, log1m_lb, nw, *, row0, n_seq, seq_len, h0, prev_out, layer_j):
    t = q.shape[0]
    blk0 = row0 // seq_len
    has_h0 = h0 is not None
    emit_state = not has_h0
    n = HGRN_HEAD_DIM
    blk = pl.BlockSpec((seq_len, n), lambda s, h: (blk0 + s, h))
    in_specs = [blk, blk, blk, blk, blk,
                pl.BlockSpec((2, n), lambda s, h: (0, h)),
                pl.BlockSpec((2, n), lambda s, h: (0, h)),
                pl.BlockSpec((1, n), lambda s, h: (0, h))]
    args = [q, ff, fb, iv, g, log_lb, log1m_lb, nw]
    if has_h0:
        in_specs.append(pl.BlockSpec((None, None, 2, None, n, n), lambda s, h: (s, layer_j, 0, h, 0, 0)))
        args.append(h0)
    io_alias = {}
    io_alias = {len(args): 0}
    in_specs.append(pl.BlockSpec(memory_space=pl.ANY))
    args.append(prev_out)
    out_specs = [pl.BlockSpec((seq_len, n), lambda s, h: (blk0 + s, h))]
    out_shape = [jax.ShapeDtypeStruct((t, HGRN_HEADS * n), BF16)]
    if emit_state:
        out_specs.append(pl.BlockSpec((None, 2, None, n, n), lambda s, h: (s, 0, h, 0, 0)))
        out_shape.append(jax.ShapeDtypeStruct((n_seq, 2, HGRN_HEADS, n, n), F32))
    kern = functools.partial(_hgrn_kernel, seq_len=seq_len, has_h0=has_h0, emit_state=emit_state)
    return pl.pallas_call(
        kern, grid=(n_seq, HGRN_HEADS), in_specs=in_specs, out_specs=out_specs, out_shape=out_shape,
        input_output_aliases=io_alias, compiler_params=_cparams(2), name="hgrn_scan",
    )(*args)


def _router_kernel(x_ref, mod_ref, nw_ref, rw_ref, rb_ref, h_ref, idx_ref, p_ref):
    h = _norm_mod(x_ref[...], nw_ref[...], mod_ref[...], 3, 4)
    h_ref[...] = h
    logits = jnp.dot(h, rw_ref[...], precision=HIGHEST, preferred_element_type=F32) + rb_ref[...]
    lane = lax.broadcasted_iota(jnp.int32, logits.shape, 1).astype(F32)
    work = logits
    vals, idxs = [], []
    for _ in range(TOP_K):
        m = jnp.max(work, axis=-1, keepdims=True)
        idx = jnp.min(jnp.where(work == m, lane, float(LANES)), axis=-1, keepdims=True)
        vals.append(m)
        idxs.append(idx)
        work = jnp.where(lane == idx, -jnp.inf, work)
    es = [jnp.exp(v - vals[0]) for v in vals]
    denom = es[0] + es[1] + es[2] + es[3]
    idx_out = jnp.zeros(logits.shape, F32)
    p_out = jnp.zeros(logits.shape, F32)
    for k in range(TOP_K):
        idx_out = jnp.where(lane == float(k), idxs[k], idx_out)
        p_out = jnp.where(lane == float(k), es[k] / denom, p_out)
    idx_ref[...] = idx_out.astype(jnp.int32)
    p_ref[...] = p_out


def _router(x, mod_l, nw, rw_pad, rb_pad, cond_map):
    t, d = x.shape
    nb = t // TOKEN_BLOCK
    row = lambda i: (i, 0)
    const = lambda i: (0, 0)
    return pl.pallas_call(
        _router_kernel,
        grid=(nb,),
        in_specs=[pl.BlockSpec((TOKEN_BLOCK, d), row),
                  pl.BlockSpec((None, 1, mod_l.shape[-1]), lambda i: (cond_map(i), 0, 0)),
                  pl.BlockSpec((1, d), const),
                  pl.BlockSpec((d, LANES), const),
                  pl.BlockSpec((1, LANES), const)],
        out_specs=[pl.BlockSpec((TOKEN_BLOCK, d), row),
                   pl.BlockSpec((TOKEN_BLOCK, LANES), row),
                   pl.BlockSpec((TOKEN_BLOCK, LANES), row)],
        out_shape=[jax.ShapeDtypeStruct((t, d), F32),
                   jax.ShapeDtypeStruct((t, LANES), jnp.int32),
                   jax.ShapeDtypeStruct((t, LANES), F32)],
        compiler_params=_cparams(1),
        name="router",
    )(x, mod_l, nw, rw_pad, rb_pad)


def _expert_kernel(te_ref, rt_ref, nu_ref, h_hbm, wg_ref, wu_ref, wd_ref, bg_ref, bu_ref, bd_ref,
                   y_ref, xbuf, wg_bf, wu_bf, wd_bf, sem):
    i = pl.program_id(0)
    tm = xbuf.shape[0]

    def row_copy(r):
        tok = rt_ref[i * tm + r]
        return pltpu.make_async_copy(h_hbm.at[pl.ds(tok, 1)], xbuf.at[pl.ds(r, 1)], sem)

    @pl.when(i < nu_ref[0])
    def _():
        def issue(r, carry):
            row_copy(r).start()
            return carry
        lax.fori_loop(0, tm, issue, 0)

        changed = jnp.logical_or(i == 0, te_ref[i] != te_ref[jnp.maximum(i - 1, 0)])

        @pl.when(changed)
        def _():
            wg_bf[...] = wg_ref[...].astype(BF16)
            wu_bf[...] = wu_ref[...].astype(BF16)
            wd_bf[...] = wd_ref[...].astype(BF16)

        def wait(r, carry):
            row_copy(r).wait()
            return carry
        lax.fori_loop(0, tm, wait, 0)

        x = xbuf[...].astype(BF16)
        gt = jnp.minimum(_dot(x, wg_bf[...]) + bg_ref[...], SWIGLU_LIMIT)
        up = jnp.clip(_dot(x, wu_bf[...]) + bu_ref[...], -SWIGLU_LIMIT, SWIGLU_LIMIT)
        act = gt * _sigmoid(SWIGLU_ALPHA * gt) * (up + 1.0)
        y_ref[...] = _dot(act.astype(BF16), wd_bf[...]) + bd_ref[...]

    @pl.when(i >= nu_ref[0])
    def _():
        y_ref[...] = jnp.zeros_like(y_ref)


def _experts(hffn, tile_expert, row_token, n_used, wg, wu, wd, bg, bu, bd, layer):
    t, d = hffn.shape
    n_tiles = tile_expert.shape[0]
    dff = wg.shape[-1]
    wspec = lambda a: pl.BlockSpec((None, None) + a.shape[2:], lambda i, te, rt, nu: (layer, te[i], 0, 0))
    grid_spec = pltpu.PrefetchScalarGridSpec(
        num_scalar_prefetch=3,
        grid=(n_tiles,),
        in_specs=[pl.BlockSpec(memory_space=pl.ANY), wspec(wg), wspec(wu), wspec(wd),
                  wspec(bg), wspec(bu), wspec(bd)],
        out_specs=pl.BlockSpec((EXPERT_TILE, d), lambda i, te, rt, nu: (i, 0)),
        scratch_shapes=[pltpu.VMEM((EXPERT_TILE, d), F32),
                        pltpu.VMEM((d, dff), BF16), pltpu.VMEM((d, dff), BF16),
                        pltpu.VMEM((dff, d), BF16),
                        pltpu.SemaphoreType.DMA(())],
    )
    return pl.pallas_call(
        _expert_kernel, grid_spec=grid_spec,
        out_shape=jax.ShapeDtypeStruct((n_tiles * EXPERT_TILE, d), F32),
        compiler_params=_cparams(1), name="experts",
    )(tile_expert, row_token, n_used, hffn, wg, wu, wd, bg, bu, bd)


def _combine_kernel(pos_ref, y_hbm, p_ref, x_ref, mod_ref, fnw_ref, o_ref, buf, sem, *, final_norm):
    i = pl.program_id(0)
    tb, d = x_ref.shape

    def row_copy(n):
        k = n // tb
        r = n - k * tb
        src = pos_ref[(i * tb + r) * TOP_K + k]
        return pltpu.make_async_copy(y_hbm.at[pl.ds(src, 1)], buf.at[k, pl.ds(r, 1)], sem)

    def issue(n, carry):
        row_copy(n).start()
        return carry
    lax.fori_loop(0, TOP_K * tb, issue, 0)

    def wait(n, carry):
        row_copy(n).wait()
        return carry
    lax.fori_loop(0, TOP_K * tb, wait, 0)

    p = p_ref[...]
    acc = p[:, 0:1] * buf[0]
    for k in range(1, TOP_K):
        acc = acc + p[:, k:k + 1] * buf[k]
    x = x_ref[...] + mod_ref[:, 5 * d:6 * d] * acc
    if final_norm:
        x = x * lax.rsqrt(jnp.mean(x * x, axis=-1, keepdims=True) + NORM_EPS) * fnw_ref[...]
    o_ref[...] = x


def _combine(pos_flat, y_sorted, top_p, x, mod_l, fnw, cond_map, final_norm):
    t, d = x.shape
    nb = t // TOKEN_BLOCK
    grid_spec = pltpu.PrefetchScalarGridSpec(
        num_scalar_prefetch=1,
        grid=(nb,),
        in_specs=[pl.BlockSpec(memory_space=pl.ANY),
                  pl.BlockSpec((TOKEN_BLOCK, LANES), lambda i, pos: (i, 0)),
                  pl.BlockSpec((TOKEN_BLOCK, d), lambda i, pos: (i, 0)),
                  pl.BlockSpec((None, 1, mod_l.shape[-1]), lambda i, pos: (cond_map(i), 0, 0)),
                  pl.BlockSpec((1, d), lambda i, pos: (0, 0))],
        out_specs=pl.BlockSpec((TOKEN_BLOCK, d), lambda i, pos: (i, 0)),
        scratch_shapes=[pltpu.VMEM((TOP_K, TOKEN_BLOCK, d), F32), pltpu.SemaphoreType.DMA(())],
    )
    return pl.pallas_call(
        functools.partial(_combine_kernel, final_norm=final_norm), grid_spec=grid_spec,
        out_shape=jax.ShapeDtypeStruct((t, d), F32),
        compiler_params=_cparams(1), name="combine",
    )(pos_flat, y_sorted, top_p, x, mod_l, fnw)


def _routing_tables(top_idx, n_tiles):
    t = top_idx.shape[0]
    ti = top_idx[:, :TOP_K]
    onehot = (ti[:, :, None] == jnp.arange(N_EXPERTS, dtype=jnp.int32)).sum(axis=1).astype(jnp.int32)
    csum = jnp.cumsum(onehot, axis=0)
    rank = csum - onehot
    counts = csum[-1]
    padded = ((counts + EXPERT_TILE - 1) // EXPERT_TILE) * EXPERT_TILE
    ends = jnp.cumsum(padded)
    offs = ends - padded
    pos = offs[ti] + jnp.take_along_axis(rank, ti, axis=1)
    tok = jnp.broadcast_to(jnp.arange(t, dtype=jnp.int32)[:, None], (t, TOP_K))
    row_token = jnp.zeros((n_tiles * EXPERT_TILE,), jnp.int32).at[pos.reshape(-1)].set(tok.reshape(-1))
    tile_start = jnp.arange(n_tiles, dtype=jnp.int32) * EXPERT_TILE
    tile_expert = jnp.minimum(jnp.searchsorted(ends, tile_start, side="right"), N_EXPERTS - 1).astype(jnp.int32)
    n_used = (ends[-1] // EXPERT_TILE).astype(jnp.int32).reshape(1)
    return tile_expert, row_token, n_used, pos.reshape(-1).astype(jnp.int32)


def _rope_tables(length):
    rows = length // GRID_W
    r = jnp.broadcast_to(jnp.arange(rows, dtype=F32)[:, None], (rows, GRID_W)).reshape(-1)
    col = jnp.broadcast_to(jnp.arange(GRID_W, dtype=F32)[None, :], (rows, GRID_W)).reshape(-1)
    n_freq = RET_QK_DIM // 4
    inv = ROPE_BASE ** (-jnp.arange(n_freq, dtype=F32) / n_freq)
    ang = jnp.concatenate([r[:, None] * inv, col[:, None] * inv], axis=-1)
    cos = jnp.repeat(jnp.cos(ang), 2, axis=-1)
    sin = jnp.repeat(jnp.sin(ang), 2, axis=-1)
    sign = jnp.where(jnp.arange(RET_QK_DIM) % 2 == 0, -1.0, 1.0).astype(F32)
    return cos, sin * sign


def _pack_pairs(s):
    lead = s.shape[:-3]
    h, n, p = s.shape[-3:]
    s = s.reshape(lead + (h // 2, 2, n, p))
    s = jnp.moveaxis(s, -3, -2)
    return s.reshape(lead + (h // 2, n, 2 * p))


def _unpack_pairs(s):
    lead = s.shape[:-3]
    hp, n, p2 = s.shape[-3:]
    s = s.reshape(lead + (hp, n, 2, p2 // 2))
    s = jnp.moveaxis(s, -2, -3)
    return s.reshape(lead + (hp * 2, n, p2 // 2))


def kernel(x_prompt, x_sample, state_ssd, state_ret, state_hgrn, c, c_ctx, mod_w, mod_b, norm_mix_w, norm_ffn_w, even_in_w, ssd_conv_w, ssd_conv_b, ssd_dt_bias, ssd_a_log, ssd_d, ssd_norm_w, ret_decay_logit, even_out_w, odd_in_w, hgrn_lower_bound, hgrn_norm_w, odd_out_w, router_w, router_b, exp_w_gate, exp_b_gate, exp_w_up, exp_b_up, exp_w_down, exp_b_down, final_norm_w):
    bp, lp, d = x_prompt.shape
    bs, ls, _ = x_sample.shape
    depth = mod_w.shape[0]
    tp, ts = bp * lp, bs * ls
    t = tp + ts
    assert lp == TOKEN_BLOCK and ls % TOKEN_BLOCK == 0 and tp % ls == 0 and d == D_MODEL
    assert 1 + bs <= COND_ROWS
    prompt_blocks = tp // TOKEN_BLOCK
    cond_map = functools.partial(_cond_row, prompt_blocks=prompt_blocks, blocks_per_request=ls // TOKEN_BLOCK)

    x = jnp.concatenate([x_prompt.reshape(tp, d), x_sample.reshape(ts, d)], axis=0)
    cond = jnp.zeros((COND_ROWS, d), F32).at[0].set(c_ctx).at[1:1 + bs].set(c)
    mod = _modulation(cond, mod_w, mod_b)
    mod = mod.reshape(depth, COND_ROWS, 1, 6 * d)

    rope = _rope_tables(ls)
    n_tiles = (t * TOP_K) // EXPERT_TILE + N_EXPERTS

    lb = jnp.cumsum(jax.nn.softmax(hgrn_lower_bound.astype(F32), axis=0), axis=0)
    lb = lb - lb[0]
    log_lb = jnp.log(lb)
    log1m_lb = jnp.log1p(-lb)

    new_ssd, new_ret, new_hgrn = [], [], []
    for layer in range(depth):
        j = layer // 2
        mod_l = mod[layer]
        nw_mix = norm_mix_w[layer].reshape(1, d)
        if layer % 2 == 0:
            w = even_in_w[j]
            o = 0
            cols = {}
            for name, n in (("z", SSD_INNER), ("xbc", SSD_CONV_CH), ("dt", 2 * SSD_HEADS), ("q", RET_QK),
                            ("k", RET_QK), ("v", RET_V), ("g", RET_V)):
                cols[name] = w[:, o:o + n]
                o += n
            dt_pad = jnp.pad(cols["dt"], ((0, 0), (0, LANES - 2 * SSD_HEADS)))
            w_r = jnp.concatenate([cols["z"], cols["xbc"], cols["q"], cols["k"], cols["v"], cols["g"], dt_pad],
                                  axis=1).astype(BF16)
            z, xbc, q, k, v, g, dt_raw = _inproj(
                x, mod_l, nw_mix, w_r,
                (SSD_INNER, SSD_CONV_CH, RET_QK, RET_QK, RET_V, RET_V, LANES),
                (F32, F32, F32, F32, BF16, F32, F32), cond_map)
            pad32 = lambda a: jnp.pad(a.reshape(1, -1).astype(F32), ((0, 0), (0, LANES - 2 * SSD_HEADS)))
            ssd_args = (xbc, dt_raw, ssd_conv_w[j], ssd_conv_b[j].reshape(1, -1), pad32(ssd_dt_bias[j]),
                        pad32(ssd_a_log[j]), jnp.repeat(ssd_d[j], SSD_HEAD_DIM).reshape(1, -1))
            y_ssd, st_ssd = _ssd_scan(*ssd_args, row0=0, n_seq=bp, seq_len=lp, h0=None,
                                      prev_out=jnp.zeros((t, SSD_INNER), F32))
            h0_ssd = _pack_pairs(state_ssd[:, j])
            (y_ssd,) = _ssd_scan(*ssd_args, row0=tp, n_seq=bs, seq_len=ls, h0=h0_ssd, prev_out=y_ssd)
            new_ssd.append(_unpack_pairs(st_ssd))
            lg_rows = jnp.broadcast_to(ret_decay_logit[j].reshape(2 * RET_HEADS, 1).astype(F32),
                                       (2 * RET_HEADS, LANES))
            y_ret, st_ret = _ret_scan(q, k, v, g, lg_rows, None, row0=0, n_seq=bp, seq_len=lp, h0=None,
                                      prev_out=jnp.zeros((t, RET_V), BF16), layer_j=j)
            (y_ret,) = _ret_scan(q, k, v, g, lg_rows, rope, row0=tp, n_seq=bs, seq_len=ls, h0=state_ret,
                                 prev_out=y_ret, layer_j=j)
            new_ret.append(st_ret)
            x = _even_out(y_ssd, z, ssd_norm_w[j].reshape(1, -1), y_ret, even_out_w[j].astype(BF16), x, mod_l,
                          cond_map)
        else:
            w_r = odd_in_w[j].astype(BF16)
            n = HGRN_HEADS * HGRN_HEAD_DIM
            q, ff, fb, iv, g = _inproj(x, mod_l, nw_mix, w_r, (n,) * 5, (F32, F32, F32, BF16, F32), cond_map)
            hargs = (q, ff, fb, iv, g, log_lb[j], log1m_lb[j], hgrn_norm_w[j].reshape(1, -1))
            o_h, st_h = _hgrn_scan(*hargs, row0=0, n_seq=bp, seq_len=lp, h0=None,
                                   prev_out=jnp.zeros((t, n), BF16), layer_j=j)
            (o_h,) = _hgrn_scan(*hargs, row0=tp, n_seq=bs, seq_len=ls, h0=state_hgrn, prev_out=o_h, layer_j=j)
            new_hgrn.append(st_h)
            x = _odd_out(o_h, odd_out_w[j].astype(BF16), x, mod_l, cond_map)

        rw_pad = jnp.pad(router_w[layer], ((0, 0), (0, LANES - N_EXPERTS)))
        rb_pad = jnp.pad(router_b[layer].reshape(1, -1), ((0, 0), (0, LANES - N_EXPERTS)), constant_values=NEG_BIG)
        hffn, top_idx, top_p = _router(x, mod_l, norm_ffn_w[layer].reshape(1, d), rw_pad, rb_pad, cond_map)
        tile_expert, row_token, n_used, pos_flat = _routing_tables(top_idx, n_tiles)
        e, _, dff = exp_b_gate.shape[1], None, exp_b_gate.shape[2]
        y_sorted = _experts(hffn, tile_expert, row_token, n_used, exp_w_gate, exp_w_up, exp_w_down,
                            exp_b_gate.reshape(depth, e, 1, dff), exp_b_up.reshape(depth, e, 1, dff),
                            exp_b_down.reshape(depth, e, 1, d), layer)
        x = _combine(pos_flat, y_sorted, top_p, x, mod_l, final_norm_w.reshape(1, d), cond_map,
                     final_norm=(layer == depth - 1))

    y_prompt = x[:tp].reshape(bp, lp, d)
    y_sample = x[tp:].reshape(bs, ls, d)
    return (y_prompt, y_sample, jnp.stack(new_ssd, axis=1), jnp.stack(new_ret, axis=1),
            jnp.stack(new_hgrn, axis=1))
```

```python
import functools
import math

import jax
import jax.numpy as jnp
from jax import lax
from jax.experimental import pallas as pl
from jax.experimental.pallas import tpu as pltpu

F32 = jnp.float32
BF16 = jnp.bfloat16
HIGHEST = lax.Precision.HIGHEST

D_MODEL = 1024
GRID_W = 64
SSD_HEADS = 16
SSD_HEAD_DIM = 64
SSD_INNER = SSD_HEADS * SSD_HEAD_DIM
SSD_STATE = 128
SSD_GROUPS = 2
SSD_CONV_CH = SSD_INNER + 2 * SSD_GROUPS * SSD_STATE
RET_HEADS = 4
RET_QK_DIM = 128
RET_V_DIM = 256
RET_QK = RET_HEADS * RET_QK_DIM
RET_V = RET_HEADS * RET_V_DIM
ROPE_BASE = 10000.0
HGRN_HEADS = 8
HGRN_HEAD_DIM = 128
N_EXPERTS = 32
TOP_K = 4
SWIGLU_LIMIT = 7.0
SWIGLU_ALPHA = 1.702
NORM_EPS = 1e-6

LANES = 128
SUBLANES = 8
TOKEN_BLOCK = 256
SCAN_CHUNK = 256
EXPERT_TILE = 256
ROW_ALIGN = 16
PACK_ROWS = -(-(TOKEN_BLOCK * TOP_K + N_EXPERTS * (ROW_ALIGN - 1)) // 256) * 256
GROUP_CHUNK_BITS = (TOKEN_BLOCK // ROW_ALIGN).bit_length()
TAIL_CHUNK_BITS = (EXPERT_TILE // ROW_ALIGN - 1).bit_length()
COND_ROWS = 16
NEG_BIG = -1e30
VMEM_LIMIT = 56 * 1024 * 1024


def _cparams(n_axes):
    return pltpu.CompilerParams(dimension_semantics=("arbitrary",) * n_axes,
                                vmem_limit_bytes=VMEM_LIMIT)


def _sigmoid(x):
    return 1.0 / (1.0 + jnp.exp(-x))


def _silu(x):
    return x * _sigmoid(x)


def _softplus(x):
    return jnp.maximum(x, 0.0) + jnp.log(1.0 + jnp.exp(-jnp.abs(x)))


def _log_sigmoid(x):
    return jnp.minimum(x, 0.0) - jnp.log(1.0 + jnp.exp(-jnp.abs(x)))


def _dot(a, b):
    return jnp.dot(a, b, preferred_element_type=F32)


def _dot_nt(a, b):
    return lax.dot_general(a, b, (((1,), (1,)), ((), ())), preferred_element_type=F32)


def _dot_tn(a, b):
    return lax.dot_general(a, b, (((0,), (0,)), ((), ())), preferred_element_type=F32)


def _cond_row(i, prompt_blocks, blocks_per_request):
    return jnp.where(i < prompt_blocks, 0, 1 + (i - prompt_blocks) // blocks_per_request)


def _mod_kernel(cond_ref, w_ref, b_ref, o_ref):
    sc = _silu(cond_ref[...]).astype(BF16)
    o_ref[...] = _dot(sc, w_ref[...].astype(BF16)) + b_ref[...]


def _modulation(cond, mod_w, mod_b):
    depth, d, n = mod_w.shape
    tn = 1536
    return pl.pallas_call(
        _mod_kernel,
        grid=(depth, n // tn),
        in_specs=[pl.BlockSpec((COND_ROWS, d), lambda l, j: (0, 0)),
                  pl.BlockSpec((None, d, tn), lambda l, j: (l, 0, j)),
                  pl.BlockSpec((None, 1, tn), lambda l, j: (l, 0, j))],
        out_specs=pl.BlockSpec((None, COND_ROWS, tn), lambda l, j: (l, 0, j)),
        out_shape=jax.ShapeDtypeStruct((depth, COND_ROWS, n), F32),
        compiler_params=_cparams(2),
        name="modulation",
    )(cond, mod_w, mod_b.reshape(depth, 1, n))


def _norm_mod(x, nw, mod, shift_idx, scale_idx):
    d = x.shape[-1]
    xn = x * lax.rsqrt(jnp.mean(x * x, axis=-1, keepdims=True) + NORM_EPS) * nw
    sh = mod[:, shift_idx * d:(shift_idx + 1) * d]
    sc = mod[:, scale_idx * d:(scale_idx + 1) * d]
    return xn * (1.0 + sc) + sh


def _inproj_kernel(x_ref, mod_ref, nw_ref, w_ref, *out_refs, splits):
    h = _norm_mod(x_ref[...], nw_ref[...], mod_ref[...], 0, 1).astype(BF16)
    off = 0
    for o_ref, n in zip(out_refs, splits):
        o_ref[...] = _dot(h, w_ref[:, off:off + n]).astype(o_ref.dtype)
        off += n


def _inproj(x, mod_l, nw, w, splits, dtypes, cond_map):
    t, d = x.shape
    n = w.shape[1]
    assert sum(splits) == n
    nb = t // TOKEN_BLOCK
    return pl.pallas_call(
        functools.partial(_inproj_kernel, splits=tuple(splits)),
        grid=(nb,),
        in_specs=[pl.BlockSpec((TOKEN_BLOCK, d), lambda i: (i, 0)),
                  pl.BlockSpec((None, 1, mod_l.shape[-1]), lambda i: (cond_map(i), 0, 0)),
                  pl.BlockSpec((1, d), lambda i: (0, 0)),
                  pl.BlockSpec((d, n), lambda i: (0, 0))],
        out_specs=[pl.BlockSpec((TOKEN_BLOCK, s), lambda i: (i, 0)) for s in splits],
        out_shape=[jax.ShapeDtypeStruct((t, s), dt) for s, dt in zip(splits, dtypes)],
        compiler_params=_cparams(1),
        name="inproj",
    )(x, mod_l, nw, w)


def _even_out_kernel(yssd_ref, z_ref, nw_ref, yret_ref, w_ref, x_ref, mod_ref, o_ref):
    d = x_ref.shape[-1]
    u = yssd_ref[...] * _silu(z_ref[...])
    un = u * lax.rsqrt(jnp.mean(u * u, axis=-1, keepdims=True) + NORM_EPS) * nw_ref[...]
    k1 = yssd_ref.shape[-1]
    acc = _dot(un.astype(BF16), w_ref[0:k1, :]) + _dot(yret_ref[...], w_ref[k1:, :])
    o_ref[...] = x_ref[...] + mod_ref[:, 2 * d:3 * d] * acc


def _even_out(yssd, z, nw, yret, w, x, mod_l, cond_map):
    t, d = x.shape
    nb = t // TOKEN_BLOCK
    row = lambda i: (i, 0)
    const = lambda i: (0, 0)
    return pl.pallas_call(
        _even_out_kernel,
        grid=(nb,),
        in_specs=[pl.BlockSpec((TOKEN_BLOCK, yssd.shape[1]), row),
                  pl.BlockSpec((TOKEN_BLOCK, z.shape[1]), row),
                  pl.BlockSpec((1, nw.shape[1]), const),
                  pl.BlockSpec((TOKEN_BLOCK, yret.shape[1]), row),
                  pl.BlockSpec(w.shape, const),
                  pl.BlockSpec((TOKEN_BLOCK, d), row),
                  pl.BlockSpec((None, 1, mod_l.shape[-1]), lambda i: (cond_map(i), 0, 0))],
        out_specs=pl.BlockSpec((TOKEN_BLOCK, d), row),
        out_shape=jax.ShapeDtypeStruct((t, d), F32),
        compiler_params=_cparams(1),
        name="even_out",
    )(yssd, z, nw, yret, w, x, mod_l)


def _odd_out_kernel(a_ref, w_ref, x_ref, mod_ref, o_ref):
    d = x_ref.shape[-1]
    o_ref[...] = x_ref[...] + mod_ref[:, 2 * d:3 * d] * _dot(a_ref[...], w_ref[...])


def _odd_out(a, w, x, mod_l, cond_map):
    t, d = x.shape
    nb = t // TOKEN_BLOCK
    row = lambda i: (i, 0)
    return pl.pallas_call(
        _odd_out_kernel,
        grid=(nb,),
        in_specs=[pl.BlockSpec((TOKEN_BLOCK, a.shape[1]), row),
                  pl.BlockSpec(w.shape, lambda i: (0, 0)),
                  pl.BlockSpec((TOKEN_BLOCK, d), row),
                  pl.BlockSpec((None, 1, mod_l.shape[-1]), lambda i: (cond_map(i), 0, 0))],
        out_specs=pl.BlockSpec((TOKEN_BLOCK, d), row),
        out_shape=jax.ShapeDtypeStruct((t, d), F32),
        compiler_params=_cparams(1),
        name="odd_out",
    )(a, w, x, mod_l)


def _tri(q, upper):
    r = lax.broadcasted_iota(jnp.int32, (q, q), 0)
    c = lax.broadcasted_iota(jnp.int32, (q, q), 1)
    return (c >= r) if upper else (c <= r)


def _cumsum_rows(x, suffix):
    q = x.shape[0]
    m = _tri(q, upper=suffix).astype(F32)
    return jnp.dot(m, x, precision=HIGHEST, preferred_element_type=F32)


def _ssd_kernel(*refs, seq_len, has_h0, emit_state):
    it = iter(refs)
    x_ref, b_ref, c_ref = next(it), next(it), next(it)
    cwx_ref, cwb_ref, cwc_ref = next(it), next(it), next(it)
    cbx_ref, cbb_ref, cbc_ref = next(it), next(it), next(it)
    dt_ref, dtb_ref, alog_ref, dskip_ref = next(it), next(it), next(it), next(it)
    h0_ref = next(it) if has_h0 else None
    next(it)
    y_ref = next(it)
    st_ref = next(it) if emit_state else None

    L = seq_len
    q = SCAN_CHUNK
    nc = L // q
    pair = pl.program_id(1)
    half = SSD_HEAD_DIM

    row = lax.broadcasted_iota(jnp.int32, (L, LANES), 0)

    def conv_silu(ref, w_ref, bias_ref):
        x = ref[...]
        w = w_ref[...]
        prev = jnp.where(row == 0, 0.0, pltpu.roll(x, 1, axis=0))
        nxt = jnp.where(row == L - 1, 0.0, pltpu.roll(x, L - 1, axis=0))
        return _silu(prev * w[0:1, :] + x * w[1:2, :] + nxt * w[2:3, :] + bias_ref[...])

    xs = conv_silu(x_ref, cwx_ref, cbx_ref)
    bm = conv_silu(b_ref, cwb_ref, cbb_ref)
    cm = conv_silu(c_ref, cwc_ref, cbc_ref)
    xs_bf, bm_bf, cm_bf = xs.astype(BF16), bm.astype(BF16), cm.astype(BF16)

    dt = _softplus(dt_ref[...] + dtb_ref[...])
    la = -dt * jnp.exp(alog_ref[...])

    lane = lax.broadcasted_iota(jnp.int32, (q, LANES), 1)
    lane_row = lax.broadcasted_iota(jnp.int32, (1, LANES), 1)
    first = lane < half
    tril = _tri(q, upper=False)
    triu = _tri(q, upper=True)

    def pick_col(arr, col):
        return jnp.sum(jnp.where(lane == col, arr, 0.0), axis=1, keepdims=True)

    def pick_row(arr_t, r):
        rr = lax.broadcasted_iota(jnp.int32, arr_t.shape, 0)
        return jnp.sum(jnp.where(rr == r, arr_t, 0.0), axis=0, keepdims=True)

    y_parts, e_f, e_b, d_sf, d_sb, dec_f, dec_b = [], [], [], [], [], [], []
    for c in range(nc):
        sl = slice(c * q, (c + 1) * q)
        la_c, dt_c = la[sl], dt[sl]
        cum = _cumsum_rows(la_c, suffix=False)
        rcs = _cumsum_rows(la_c, suffix=True)
        cum_t, rcs_t, dt_t = cum.T, rcs.T, dt_c.T
        s_g = _dot_nt(cm_bf[sl], bm_bf[sl])
        y_c = None
        cols = {}
        for hh in range(2):
            head = 2 * pair + hh
            cf, cb = pick_col(cum, head), pick_col(rcs, SSD_HEADS + head)
            rf, rb = pick_row(cum_t, head), pick_row(rcs_t, SSD_HEADS + head)
            dtf_r, dtb_r = pick_row(dt_t, head), pick_row(dt_t, SSD_HEADS + head)
            dec = (jnp.exp(jnp.where(tril, cf - rf, NEG_BIG)) * dtf_r
                   + jnp.exp(jnp.where(triu, cb - rb, NEG_BIG)) * dtb_r)
            y_h = _dot((s_g * dec).astype(BF16), xs_bf[sl])
            y_c = y_h if hh == 0 else jnp.where(first, y_c, y_h)
            cols[hh] = (cf, cb, pick_col(dt_c, head), pick_col(dt_c, SSD_HEADS + head))
        cum_e = jnp.where(first, cols[0][0], cols[1][0])
        rcs_e = jnp.where(first, cols[0][1], cols[1][1])
        dtf_e = jnp.where(first, cols[0][2], cols[1][2])
        dtb_e = jnp.where(first, cols[0][3], cols[1][3])
        y_parts.append(y_c)
        e_f.append(jnp.exp(cum_e))
        e_b.append(jnp.exp(rcs_e))
        xs_c = xs[sl]
        wf = xs_c * dtf_e * jnp.exp(cum_e[q - 1:q, :] - cum_e)
        wb = xs_c * dtb_e * jnp.exp(rcs_e[0:1, :] - rcs_e)
        d_sf.append(_dot_tn(bm_bf[sl], wf.astype(BF16)))
        d_sb.append(_dot_tn(bm_bf[sl], wb.astype(BF16)))
        dec_f.append(jnp.exp(cum_e[q - 1:q, :]))
        dec_b.append(jnp.exp(rcs_e[0:1, :]))

    zero = jnp.zeros((SSD_STATE, LANES), F32)
    sf = h0_ref[0] if has_h0 else zero
    sf_in = []
    for c in range(nc):
        sf_in.append(sf)
        sf = dec_f[c] * sf + d_sf[c]
    sb = h0_ref[1] if has_h0 else zero
    sb_in = [None] * nc
    for c in reversed(range(nc)):
        sb_in[c] = sb
        sb = dec_b[c] * sb + d_sb[c]

    dsk = dskip_ref[...]
    for c in range(nc):
        sl = slice(c * q, (c + 1) * q)
        y_c = y_parts[c] + xs[sl] * dsk
        if has_h0 or c > 0:
            y_c = y_c + e_f[c] * _dot(cm_bf[sl], sf_in[c].astype(BF16))
        if has_h0 or c < nc - 1:
            y_c = y_c + e_b[c] * _dot(cm_bf[sl], sb_in[c].astype(BF16))
        y_ref[sl, :] = y_c
    if emit_state:
        st_ref[0] = sf
        st_ref[1] = sb
    del lane_row


def _ssd_scan(xbc, dt_raw, conv_w, conv_b, dt_bias, a_log, d_skip, *, row0, n_seq, seq_len,
              h0, prev_out):
    t = xbc.shape[0]
    blk0 = row0 // seq_len
    n_pairs = SSD_HEADS // 2
    pairs_per_group = n_pairs // SSD_GROUPS
    xoff, boff, coff = 0, SSD_INNER // LANES, (SSD_INNER + SSD_GROUPS * SSD_STATE) // LANES
    has_h0 = h0 is not None
    emit_state = not has_h0

    def seq_blk(col_fn):
        return lambda s, p: (blk0 + s, col_fn(p))

    fx = lambda p: xoff + p
    fb = lambda p: boff + p // pairs_per_group
    fc = lambda p: coff + p // pairs_per_group
    vec = lambda rows, col_fn: pl.BlockSpec((rows, LANES), lambda s, p: (0, col_fn(p)))
    in_specs = [pl.BlockSpec((seq_len, LANES), seq_blk(fx)),
                pl.BlockSpec((seq_len, LANES), seq_blk(fb)),
                pl.BlockSpec((seq_len, LANES), seq_blk(fc)),
                vec(3, fx), vec(3, fb), vec(3, fc),
                vec(1, fx), vec(1, fb), vec(1, fc),
                pl.BlockSpec((seq_len, LANES), lambda s, p: (blk0 + s, 0)),
                pl.BlockSpec((1, LANES), lambda s, p: (0, 0)),
                pl.BlockSpec((1, LANES), lambda s, p: (0, 0)),
                vec(1, fx)]
    args = [xbc, xbc, xbc, conv_w, conv_w, conv_w, conv_b, conv_b, conv_b,
            dt_raw, dt_bias, a_log, d_skip]
    io_alias = {}
    if has_h0:
        in_specs.append(pl.BlockSpec((None, 2, None, SSD_STATE, LANES), lambda s, p: (s, 0, p, 0, 0)))
        args.append(h0)
    io_alias = {len(args): 0}
    in_specs.append(pl.BlockSpec(memory_space=pl.ANY))
    args.append(prev_out)
    out_specs = [pl.BlockSpec((seq_len, LANES), lambda s, p: (blk0 + s, p))]
    out_shape = [jax.ShapeDtypeStruct((t, SSD_INNER), F32)]
    if emit_state:
        out_specs.append(pl.BlockSpec((None, 2, None, SSD_STATE, LANES), lambda s, p: (s, 0, p, 0, 0)))
        out_shape.append(jax.ShapeDtypeStruct((n_seq, 2, n_pairs, SSD_STATE, LANES), F32))
    kern = functools.partial(_ssd_kernel, seq_len=seq_len, has_h0=has_h0, emit_state=emit_state)
    return pl.pallas_call(
        kern, grid=(n_seq, n_pairs), in_specs=in_specs, out_specs=out_specs, out_shape=out_shape,
        input_output_aliases=io_alias, compiler_params=_cparams(2), name="ssd_scan",
    )(*args)


def _ret_kernel(*refs, seq_len, has_h0, emit_state, use_rope):
    it = iter(refs)
    q_ref, k_ref, v_ref, g_ref, lg_ref = next(it), next(it), next(it), next(it), next(it)
    cos_ref = next(it) if use_rope else None
    sin_ref = next(it) if use_rope else None
    h0_ref = next(it) if has_h0 else None
    next(it)
    o_ref = next(it)
    st_ref = next(it) if emit_state else None

    L = seq_len
    qn = SCAN_CHUNK
    nc = L // qn
    head = pl.program_id(1)

    qf = q_ref[...]
    kf = k_ref[...] * (RET_QK_DIM ** -0.5)
    if use_rope:
        lane = lax.broadcasted_iota(jnp.int32, (L, LANES), 1)
        even = (lane % 2) == 0
        cos, sin = cos_ref[...], sin_ref[...]

        def rope(x):
            swapped = jnp.where(even, pltpu.roll(x, LANES - 1, axis=1), pltpu.roll(x, 1, axis=1))
            return x * cos + swapped * sin
        qf, kf = rope(qf), rope(kf)
    q_bf, k_bf = qf.astype(BF16), kf.astype(BF16)
    v_bf = v_ref[...]

    lg = _log_sigmoid(lg_ref[...])
    rr = lax.broadcasted_iota(jnp.int32, lg.shape, 0)
    lgf = jnp.sum(jnp.where(rr == head, lg, 0.0), axis=0, keepdims=True)[:, 0:1]
    lgb = jnp.sum(jnp.where(rr == RET_HEADS + head, lg, 0.0), axis=0, keepdims=True)[:, 0:1]

    ti = lax.broadcasted_iota(jnp.int32, (qn, qn), 0)
    si = lax.broadcasted_iota(jnp.int32, (qn, qn), 1)
    dist = (ti - si).astype(F32)
    dec = (jnp.exp(jnp.where(ti >= si, dist * lgf, NEG_BIG))
           + jnp.exp(jnp.where(si >= ti, -dist * lgb, NEG_BIG)))
    tcol = lax.broadcasted_iota(jnp.int32, (qn, 1), 0).astype(F32)
    ef = jnp.exp((tcol + 1.0) * lgf)
    eb = jnp.exp((qn - tcol) * lgb)
    wf = jnp.exp((qn - 1.0 - tcol) * lgf)
    wb = jnp.exp(tcol * lgb)
    dec_f = jnp.exp(qn * lgf)
    dec_b = jnp.exp(qn * lgb)

    y_parts, d_sf, d_sb = [], [], []
    for c in range(nc):
        sl = slice(c * qn, (c + 1) * qn)
        s = _dot_nt(q_bf[sl], k_bf[sl])
        y_parts.append(_dot((s * dec).astype(BF16), v_bf[sl]))
        d_sf.append(_dot_tn((kf[sl] * wf).astype(BF16), v_bf[sl]))
        d_sb.append(_dot_tn((kf[sl] * wb).astype(BF16), v_bf[sl]))

    zero = jnp.zeros((RET_QK_DIM, RET_V_DIM), F32)
    sf = h0_ref[0] if has_h0 else zero
    sf_in = []
    for c in range(nc):
        sf_in.append(sf)
        sf = dec_f * sf + d_sf[c]
    sb = h0_ref[1] if has_h0 else zero
    sb_in = [None] * nc
    for c in reversed(range(nc)):
        sb_in[c] = sb
        sb = dec_b * sb + d_sb[c]

    for c in range(nc):
        sl = slice(c * qn, (c + 1) * qn)
        o = y_parts[c]
        if has_h0 or c > 0:
            o = o + ef * _dot(q_bf[sl], sf_in[c].astype(BF16))
        if has_h0 or c < nc - 1:
            o = o + eb * _dot(q_bf[sl], sb_in[c].astype(BF16))
        mu = jnp.mean(o, axis=-1, keepdims=True)
        var = jnp.mean(jnp.square(o - mu), axis=-1, keepdims=True)
        on = (o - mu) * lax.rsqrt(var + NORM_EPS)
        o_ref[sl, :] = (on * _silu(g_ref[sl, :])).astype(o_ref.dtype)
    if emit_state:
        st_ref[0] = sf
        st_ref[1] = sb


def _ret_scan(q, k, v, g, lg_rows, rope, *, row0, n_seq, seq_len, h0, prev_out, layer_j):
    t = q.shape[0]
    blk0 = row0 // seq_len
    has_h0 = h0 is not None
    emit_state = not has_h0
    use_rope = rope is not None
    blk = lambda w: pl.BlockSpec((seq_len, w), lambda s, h: (blk0 + s, h))
    in_specs = [blk(RET_QK_DIM), blk(RET_QK_DIM), blk(RET_V_DIM), blk(RET_V_DIM),
                pl.BlockSpec(lg_rows.shape, lambda s, h: (0, 0))]
    args = [q, k, v, g, lg_rows]
    if use_rope:
        in_specs += [pl.BlockSpec((seq_len, LANES), lambda s, h: (0, 0))] * 2
        args += [rope[0], rope[1]]
    if has_h0:
        in_specs.append(pl.BlockSpec((None, None, 2, None, RET_QK_DIM, RET_V_DIM),
                                     lambda s, h: (s, layer_j, 0, h, 0, 0)))
        args.append(h0)
    io_alias = {}
    io_alias = {len(args): 0}
    in_specs.append(pl.BlockSpec(memory_space=pl.ANY))
    args.append(prev_out)
    out_specs = [pl.BlockSpec((seq_len, RET_V_DIM), lambda s, h: (blk0 + s, h))]
    out_shape = [jax.ShapeDtypeStruct((t, RET_V), BF16)]
    if emit_state:
        out_specs.append(pl.BlockSpec((None, 2, None, RET_QK_DIM, RET_V_DIM), lambda s, h: (s, 0, h, 0, 0)))
        out_shape.append(jax.ShapeDtypeStruct((n_seq, 2, RET_HEADS, RET_QK_DIM, RET_V_DIM), F32))
    kern = functools.partial(_ret_kernel, seq_len=seq_len, has_h0=has_h0, emit_state=emit_state,
                             use_rope=use_rope)
    return pl.pallas_call(
        kern, grid=(n_seq, RET_HEADS), in_specs=in_specs, out_specs=out_specs, out_shape=out_shape,
        input_output_aliases=io_alias, compiler_params=_cparams(2), name="ret_scan",
    )(*args)


def _pair_ref(arr, h, off):
    qn, n = arr.shape
    w = 2 * h
    if w >= 8:
        a3 = arr.reshape(qn // w, w, n)
        return jnp.broadcast_to(a3[:, off:off + 1, :], (qn // w, w, n)).reshape(qn, n)
    t = lax.broadcasted_iota(jnp.int32, arr.shape, 0)
    out = jnp.zeros_like(arr)
    for r in range(w):
        out = jnp.where((t % w) == r, pltpu.roll(arr, (r - off) % qn, axis=0), out)
    return out


def _hgrn_kernel(*refs, seq_len, has_h0, emit_state):
    it = iter(refs)
    q_ref, ff_ref, fb_ref, i_ref, g_ref = next(it), next(it), next(it), next(it), next(it)
    llb_ref, l1m_ref, nw_ref = next(it), next(it), next(it)
    h0_ref = next(it) if has_h0 else None
    next(it)
    o_ref = next(it)
    st_ref = next(it) if emit_state else None

    L = seq_len
    qn = SCAN_CHUNK
    nc = L // qn

    def log_forget(fz, d):
        a = llb_ref[d:d + 1, :]
        b = l1m_ref[d:d + 1, :] + _log_sigmoid(fz)
        m = jnp.maximum(a, b)
        return m + jnp.log(1.0 + jnp.exp(-jnp.abs(a - b)))

    lf_f = log_forget(ff_ref[...], 0)
    lf_b = log_forget(fb_ref[...], 1)
    k_f = 1.0 - jnp.exp(lf_f)
    k_b = 1.0 - jnp.exp(lf_b)
    qv = q_ref[...]
    v_bf = i_ref[...]

    t_col = lax.broadcasted_iota(jnp.int32, (qn, 1), 0)
    ti = lax.broadcasted_iota(jnp.int32, (qn, qn), 0)
    si = lax.broadcasted_iota(jnp.int32, (qn, qn), 1)

    y_parts, e_f, e_b, d_sf, d_sb, dec_f, dec_b = [], [], [], [], [], [], []
    for c in range(nc):
        sl = slice(c * qn, (c + 1) * qn)
        q_c, kf_c, kb_c = qv[sl], k_f[sl], k_b[sl]
        cum = _cumsum_rows(lf_f[sl], suffix=False)
        rcs = _cumsum_rows(lf_b[sl], suffix=True)
        scores = jnp.zeros((qn, qn), F32)
        h = 1
        while h < qn:
            upper = ((t_col // h) % 2) == 1
            ref_f = _pair_ref(cum, h, h - 1)
            ref_b = _pair_ref(rcs, h, h)
            aq = jnp.exp(jnp.where(upper, cum - ref_f, rcs - ref_b))
            ak = jnp.exp(jnp.where(upper, ref_b - rcs, ref_f - cum))
            qt = (q_c * aq).astype(BF16)
            kt = (jnp.where(upper, kb_c, kf_c) * ak).astype(BF16)
            s_l = _dot_nt(qt, kt)
            mask = ((ti // (2 * h)) == (si // (2 * h))) & (((ti // h) % 2) != ((si // h) % 2))
            scores = jnp.where(mask, s_l, scores)
            h *= 2
        diag = jnp.sum(q_c * (kf_c + kb_c), axis=-1, keepdims=True)
        y_parts.append(_dot(scores.astype(BF16), v_bf[sl]) + diag * v_bf[sl].astype(F32))
        e_f.append(jnp.exp(cum))
        e_b.append(jnp.exp(rcs))
        d_sf.append(_dot_tn((kf_c * jnp.exp(cum[qn - 1:qn, :] - cum)).astype(BF16), v_bf[sl]))
        d_sb.append(_dot_tn((kb_c * jnp.exp(rcs[0:1, :] - rcs)).astype(BF16), v_bf[sl]))
        n = HGRN_HEAD_DIM
        dec_f.append(jnp.broadcast_to(jnp.exp(cum[qn - 1:qn, :]), (n, n)).T)
        dec_b.append(jnp.broadcast_to(jnp.exp(rcs[0:1, :]), (n, n)).T)

    zero = jnp.zeros((HGRN_HEAD_DIM, HGRN_HEAD_DIM), F32)
    sf = h0_ref[0] if has_h0 else zero
    sf_in = []
    for c in range(nc):
        sf_in.append(sf)
        sf = dec_f[c] * sf + d_sf[c]
    sb = h0_ref[1] if has_h0 else zero
    sb_in = [None] * nc
    for c in reversed(range(nc)):
        sb_in[c] = sb
        sb = dec_b[c] * sb + d_sb[c]

    nw = nw_ref[...]
    for c in range(nc):
        sl = slice(c * qn, (c + 1) * qn)
        o = y_parts[c]
        if has_h0 or c > 0:
            o = o + _dot((qv[sl] * e_f[c]).astype(BF16), sf_in[c].astype(BF16))
        if has_h0 or c < nc - 1:
            o = o + _dot((qv[sl] * e_b[c]).astype(BF16), sb_in[c].astype(BF16))
        on = o * lax.rsqrt(jnp.mean(o * o, axis=-1, keepdims=True) + NORM_EPS) * nw
        o_ref[sl, :] = (on * _silu(g_ref[sl, :])).astype(o_ref.dtype)
    if emit_state:
        st_ref[0] = sf
        st_ref[1] = sb


def _hgrn_scan(q, ff, fb, iv, g, log_lb, log1m_lb, nw, *, row0, n_seq, seq_len, h0, prev_out, layer_j):
    t = q.shape[0]
    blk0 = row0 // seq_len
    has_h0 = h0 is not None
    emit_state = not has_h0
    n = HGRN_HEAD_DIM
    blk = pl.BlockSpec((seq_len, n), lambda s, h: (blk0 + s, h))
    in_specs = [blk, blk, blk, blk, blk,
                pl.BlockSpec((2, n), lambda s, h: (0, h)),
                pl.BlockSpec((2, n), lambda s, h: (0, h)),
                pl.BlockSpec((1, n), lambda s, h: (0, h))]
    args = [q, ff, fb, iv, g, log_lb, log1m_lb, nw]
    if has_h0:
        in_specs.append(pl.BlockSpec((None, None, 2, None, n, n), lambda s, h: (s, layer_j, 0, h, 0, 0)))
        args.append(h0)
    io_alias = {}
    io_alias = {len(args): 0}
    in_specs.append(pl.BlockSpec(memory_space=pl.ANY))
    args.append(prev_out)
    out_specs = [pl.BlockSpec((seq_len, n), lambda s, h: (blk0 + s, h))]
    out_shape = [jax.ShapeDtypeStruct((t, HGRN_HEADS * n), BF16)]
    if emit_state:
        out_specs.append(pl.BlockSpec((None, 2, None, n, n), lambda s, h: (s, 0, h, 0, 0)))
        out_shape.append(jax.ShapeDtypeStruct((n_seq, 2, HGRN_HEADS, n, n), F32))
    kern = functools.partial(_hgrn_kernel, seq_len=seq_len, has_h0=has_h0, emit_state=emit_state)
    return pl.pallas_call(
        kern, grid=(n_seq, HGRN_HEADS), in_specs=in_specs, out_specs=out_specs, out_shape=out_shape,
        input_output_aliases=io_alias, compiler_params=_cparams(2), name="hgrn_scan",
    )(*args)


def _router_kernel(x_ref, mod_ref, nw_ref, rw_ref, rb_ref, h_ref, gate_ref, rank_ref, cnt_ref):
    h = _norm_mod(x_ref[...], nw_ref[...], mod_ref[...], 3, 4)
    h_ref[...] = h.astype(BF16)
    logits = jnp.dot(h, rw_ref[...], precision=HIGHEST, preferred_element_type=F32) + rb_ref[...]
    lane = lax.broadcasted_iota(jnp.int32, logits.shape, 1).astype(F32)
    work = logits
    vals, idxs = [], []
    for _ in range(TOP_K):
        m = jnp.max(work, axis=-1, keepdims=True)
        idx = jnp.min(jnp.where(work == m, lane, float(LANES)), axis=-1, keepdims=True)
        vals.append(m)
        idxs.append(idx)
        work = jnp.where(lane == idx, -jnp.inf, work)
    es = [jnp.exp(v - vals[0]) for v in vals]
    denom = es[0] + es[1] + es[2] + es[3]
    gates = jnp.zeros(logits.shape, F32)
    member = jnp.zeros(logits.shape, F32)
    for k in range(TOP_K):
        hit = lane == idxs[k]
        gates = jnp.where(hit, es[k] / denom, gates)
        member = jnp.where(hit, 1.0, member)
    tb = logits.shape[0]
    before = _tri(tb, upper=False) & ~_tri(tb, upper=True)
    rank = _dot(before.astype(BF16), member.astype(BF16))
    gate_ref[...] = gates
    rank_ref[...] = jnp.where(member > 0.0, rank, -1.0)
    cnt_ref[...] = jnp.sum(member, axis=0, keepdims=True).astype(jnp.int32)


def _router(x, mod_l, nw, rw_pad, rb_pad, cond_map):
    t, d = x.shape
    nb = t // TOKEN_BLOCK
    row = lambda i: (i, 0)
    const = lambda i: (0, 0)
    return pl.pallas_call(
        _router_kernel,
        grid=(nb,),
        in_specs=[pl.BlockSpec((TOKEN_BLOCK, d), row),
                  pl.BlockSpec((None, 1, mod_l.shape[-1]), lambda i: (cond_map(i), 0, 0)),
                  pl.BlockSpec((1, d), const),
                  pl.BlockSpec((d, LANES), const),
                  pl.BlockSpec((1, LANES), const)],
        out_specs=[pl.BlockSpec((TOKEN_BLOCK, d), row),
                   pl.BlockSpec((TOKEN_BLOCK, LANES), row),
                   pl.BlockSpec((TOKEN_BLOCK, LANES), row),
                   pl.BlockSpec((None, 1, LANES), lambda i: (i, 0, 0))],
        out_shape=[jax.ShapeDtypeStruct((t, d), BF16),
                   jax.ShapeDtypeStruct((t, LANES), F32),
                   jax.ShapeDtypeStruct((t, LANES), F32),
                   jax.ShapeDtypeStruct((nb, 1, LANES), jnp.int32)],
        compiler_params=_cparams(1),
        name="router",
    )(x, mod_l, nw, rw_pad, rb_pad)


def _split2(x):
    hi = x.astype(BF16)
    return hi, (x - hi.astype(F32)).astype(BF16)


def _split3(x):
    hi, mid = _split2(x)
    r2 = x - hi.astype(F32) - mid.astype(F32)
    return hi, mid, r2.astype(BF16)


def _packed_owner(lo_row, hi_row):
    r = lax.broadcasted_iota(jnp.int32, (PACK_ROWS, LANES), 0).astype(F32)
    return (r >= lo_row) & (r < hi_row)


def _chunks(units, bits):
    for b in range(bits):
        yield (((units >> b) & 1) == 1,
               pl.multiple_of(((units >> (b + 1)) << (b + 1)) * ROW_ALIGN, ROW_ALIGN),
               ROW_ALIGN << b)


def _group_chunks(i, e, start_ref, lo_ref, units_ref):
    idx = i * N_EXPERTS + e
    for present, base, rows in _chunks(units_ref[idx], GROUP_CHUNK_BITS):
        yield (present, pl.multiple_of(lo_ref[idx] + base, ROW_ALIGN),
               pl.multiple_of(start_ref[idx] + base, ROW_ALIGN), rows)


def _zero_unused_rows(tail_start_ref, tail_units_ref, nu_ref, xs_hbm, zeros_buf, sem):
    n_tiles = xs_hbm.shape[0] // EXPERT_TILE
    zeros_buf[...] = jnp.zeros_like(zeros_buf)

    def tail_copy(off, rows):
        return pltpu.make_async_copy(zeros_buf.at[pl.ds(0, rows)], xs_hbm.at[pl.ds(off, rows)], sem)

    def tile_copy(n):
        return pltpu.make_async_copy(zeros_buf, xs_hbm.at[pl.ds(pl.multiple_of(n * EXPERT_TILE, EXPERT_TILE),
                                                                EXPERT_TILE)], sem)

    def each(fn):
        for e in range(N_EXPERTS):
            for present, base, rows in _chunks(tail_units_ref[e], TAIL_CHUNK_BITS):
                @pl.when(present)
                def _():
                    fn(tail_copy(pl.multiple_of(tail_start_ref[e] + base, ROW_ALIGN), rows))

        def body(n, carry):
            fn(tile_copy(n))
            return carry
        lax.fori_loop(nu_ref[0], n_tiles, body, 0)

    each(lambda c: c.start())
    each(lambda c: c.wait())


def _dispatch_kernel(start_ref, lo_ref, units_ref, tail_start_ref, tail_units_ref, nu_ref,
                     h_ref, rank_ref, lorow_ref, hirow_ref, xs_hbm, xs_buf, zeros_buf, sem):
    i = pl.program_id(0)

    @pl.when(i == 0)
    def _():
        _zero_unused_rows(tail_start_ref, tail_units_ref, nu_ref, xs_hbm, zeros_buf, sem)

    lo_row = lorow_ref[...]
    owner = _packed_owner(lo_row, hirow_ref[...])
    owned = jnp.sum(owner.astype(F32), axis=1, keepdims=True) > 0.0
    r = lax.broadcasted_iota(jnp.int32, (PACK_ROWS, 1), 0).astype(F32)
    local = r - jnp.sum(jnp.where(owner, lo_row, 0.0), axis=1, keepdims=True)
    rank_rows = _dot_nt(owner.astype(BF16), rank_ref[...].astype(BF16))
    sel = ((rank_rows == local) & owned).astype(BF16)
    xs_buf[...] = _dot(sel, h_ref[...]).astype(BF16)

    def copy(v_off, s_off, rows):
        return pltpu.make_async_copy(xs_buf.at[pl.ds(v_off, rows)], xs_hbm.at[pl.ds(s_off, rows)], sem)

    for e in range(N_EXPERTS):
        for present, v_off, s_off, rows in _group_chunks(i, e, start_ref, lo_ref, units_ref):
            @pl.when(present)
            def _():
                copy(v_off, s_off, rows).start()
    for e in range(N_EXPERTS):
        for present, v_off, s_off, rows in _group_chunks(i, e, start_ref, lo_ref, units_ref):
            @pl.when(present)
            def _():
                copy(v_off, s_off, rows).wait()


def _dispatch(hffn, rank, tables, n_rows):
    t, d = hffn.shape
    nb = t // TOKEN_BLOCK
    row = lambda i, *_: (i, 0)
    vec = pl.BlockSpec((None, 1, LANES), lambda i, *_: (i, 0, 0))
    grid_spec = pltpu.PrefetchScalarGridSpec(
        num_scalar_prefetch=6,
        grid=(nb,),
        in_specs=[pl.BlockSpec((TOKEN_BLOCK, d), row), pl.BlockSpec((TOKEN_BLOCK, LANES), row), vec, vec],
        out_specs=pl.BlockSpec(memory_space=pl.ANY),
        scratch_shapes=[pltpu.VMEM((PACK_ROWS, d), BF16), pltpu.VMEM((EXPERT_TILE, d), BF16),
                        pltpu.SemaphoreType.DMA(())],
    )
    return pl.pallas_call(
        _dispatch_kernel, grid_spec=grid_spec,
        out_shape=jax.ShapeDtypeStruct((n_rows, d), BF16),
        compiler_params=_cparams(1), name="dispatch",
    )(tables["start"], tables["lo"], tables["units"], tables["tail_start"], tables["tail_units"],
      tables["n_used"], hffn, rank, tables["lo_row"], tables["hi_row"])


def _expert_kernel(te_ref, nu_ref, x_ref, wg_ref, wu_ref, wd_ref, bg_ref, bu_ref, bd_ref,
                   y_ref, wg_bf, wu_bf, wd_bf):
    i = pl.program_id(0)

    @pl.when(i < nu_ref[0])
    def _():
        changed = jnp.logical_or(i == 0, te_ref[i] != te_ref[jnp.maximum(i - 1, 0)])

        @pl.when(changed)
        def _():
            wg_bf[...] = wg_ref[...].astype(BF16)
            wu_bf[...] = wu_ref[...].astype(BF16)
            wd_bf[...] = wd_ref[...].astype(BF16)

        x = x_ref[...]
        gt =jnp.minimum(_dot(x, wg_bf[...]) + bg_ref[...], SWIGLU_LIMIT)
        up = jnp.clip(_dot(x, wu_bf[...]) + bu_ref[...], -SWIGLU_LIMIT, SWIGLU_LIMIT)
        act = gt * _sigmoid(SWIGLU_ALPHA * gt) * (up + 1.0)
        y_ref[...] = (_dot(act.astype(BF16), wd_bf[...]) + bd_ref[...]).astype(y_ref.dtype)

    @pl.when(i >= nu_ref[0])
    def _():
        y_ref[...] = jnp.zeros_like(y_ref)


def _experts(x_sorted, tile_expert, n_used, wg, wu, wd, bg, bu, bd, layer):
    n_rows, d = x_sorted.shape
    n_tiles = n_rows // EXPERT_TILE
    dff = wg.shape[-1]
    wspec = lambda a: pl.BlockSpec((None, None) + a.shape[2:], lambda i, te, nu: (layer, te[i], 0, 0))
    row = lambda i, te, nu: (i, 0)
    grid_spec = pltpu.PrefetchScalarGridSpec(
        num_scalar_prefetch=2,
        grid=(n_tiles,),
        in_specs=[pl.BlockSpec((EXPERT_TILE, d), row),
                  wspec(wg), wspec(wu), wspec(wd), wspec(bg), wspec(bu), wspec(bd)],
        out_specs=pl.BlockSpec((EXPERT_TILE, d), row),
        scratch_shapes=[pltpu.VMEM((d, dff), BF16), pltpu.VMEM((d, dff), BF16), pltpu.VMEM((dff, d), BF16)],
    )
    return pl.pallas_call(
        _expert_kernel, grid_spec=grid_spec,
        out_shape=jax.ShapeDtypeStruct((n_rows, d), BF16),
        compiler_params=_cparams(1), name="experts",
    )(tile_expert, n_used, x_sorted, wg, wu, wd, bg, bu, bd)


def _combine_kernel(start_ref, lo_ref, units_ref, y_hbm, rank_ref, gate_ref, lorow_ref, hirow_ref,
                    x_ref, mod_ref, fnw_ref, o_ref, buf, sem, *, final_norm):
    i = pl.program_id(0)
    d = x_ref.shape[-1]

    @pl.when(i == 0)
    def _():
        buf[...] = jnp.zeros_like(buf)

    def copy(v_off, s_off, rows):
        return pltpu.make_async_copy(y_hbm.at[pl.ds(s_off, rows)], buf.at[pl.ds(v_off, rows)], sem)

    for e in range(N_EXPERTS):
        for present, v_off, s_off, rows in _group_chunks(i, e, start_ref, lo_ref, units_ref):
            @pl.when(present)
            def _():
                copy(v_off, s_off, rows).start()

    lo_row = lorow_ref[...]
    owner = _packed_owner(lo_row, hirow_ref[...]).astype(BF16)
    rank_cols = _dot_nt(rank_ref[...].astype(BF16), owner)
    lo8 = jnp.broadcast_to(lo_row, (SUBLANES, LANES))
    lo_cols = sum(_dot_nt(part, owner) for part in _split3(lo8))[0:1, :]
    owned = _dot_nt(jnp.ones((SUBLANES, LANES), BF16), owner)[0:1, :] > 0.0
    col = lax.broadcasted_iota(jnp.int32, (1, PACK_ROWS), 1).astype(F32)
    match = (rank_cols == col - lo_cols) & owned
    g_hi, g_lo = _split2(gate_ref[...])
    q_hi = jnp.where(match, _dot_nt(g_hi, owner), 0.0).astype(BF16)
    q_lo = jnp.where(match, _dot_nt(g_lo, owner), 0.0).astype(BF16)

    for e in range(N_EXPERTS):
        for present, v_off, s_off, rows in _group_chunks(i, e, start_ref, lo_ref, units_ref):
            @pl.when(present)
            def _():
                copy(v_off, s_off, rows).wait()
    y = buf[...]
    x = x_ref[...] + mod_ref[:, 5 * d:6 * d] * (_dot(q_hi, y) + _dot(q_lo, y))
    if final_norm:
        x = x * lax.rsqrt(jnp.mean(x * x, axis=-1, keepdims=True) + NORM_EPS) * fnw_ref[...]
    o_ref[...] = x


def _combine(y_sorted, rank, gates, tables, x, mod_l, fnw, cond_map, final_norm):
    t, d = x.shape
    nb = t // TOKEN_BLOCK
    row = lambda i, *_: (i, 0)
    vec = pl.BlockSpec((None, 1, LANES), lambda i, *_: (i, 0, 0))
    grid_spec = pltpu.PrefetchScalarGridSpec(
        num_scalar_prefetch=3,
        grid=(nb,),
        in_specs=[pl.BlockSpec(memory_space=pl.ANY),
                  pl.BlockSpec((TOKEN_BLOCK, LANES), row),
                  pl.BlockSpec((TOKEN_BLOCK, LANES), row),
                  vec, vec,
                  pl.BlockSpec((TOKEN_BLOCK, d), row),
                  pl.BlockSpec((None, 1, mod_l.shape[-1]), lambda i, *_: (cond_map(i), 0, 0)),
                  pl.BlockSpec((1, d), lambda i, *_: (0, 0))],
        out_specs=pl.BlockSpec((TOKEN_BLOCK, d), row),
        scratch_shapes=[pltpu.VMEM((PACK_ROWS, d), BF16), pltpu.SemaphoreType.DMA(())],
    )
    return pl.pallas_call(
        functools.partial(_combine_kernel, final_norm=final_norm), grid_spec=grid_spec,
        out_shape=jax.ShapeDtypeStruct((t, d), F32),
        compiler_params=_cparams(1), name="combine",
    )(tables["start"], tables["lo"], tables["units"], y_sorted, rank, gates, tables["lo_row"],
      tables["hi_row"], x, mod_l, fnw)


def _routing_tables(cnt_tiles, n_tiles):
    nb = cnt_tiles.shape[0]
    c = cnt_tiles[:, 0, :N_EXPERTS]
    ca = ((c + ROW_ALIGN - 1) // ROW_ALIGN) * ROW_ALIGN
    lo = jnp.cumsum(ca, axis=1) - ca
    tot = ca.sum(axis=0)
    seg = ((tot + EXPERT_TILE - 1) // EXPERT_TILE) * EXPERT_TILE
    ends = jnp.cumsum(seg)
    offs = ends - seg
    start = offs[None, :] + jnp.cumsum(ca, axis=0) - ca
    tile_start = jnp.arange(n_tiles, dtype=jnp.int32) * EXPERT_TILE
    tile_expert = jnp.minimum((tile_start[:, None] >= ends[None, :]).sum(axis=1), N_EXPERTS - 1)
    i32 = lambda a: a.astype(jnp.int32)
    row = lambda a: jnp.pad(a.astype(F32), ((0, 0), (0, LANES - N_EXPERTS))).reshape(nb, 1, LANES)
    return dict(start=i32(start.reshape(-1)), lo=i32(lo.reshape(-1)), units=i32((ca // ROW_ALIGN).reshape(-1)),
                tail_start=i32(offs + tot), tail_units=i32((seg - tot) // ROW_ALIGN),
                n_used=i32((ends[-1] // EXPERT_TILE).reshape(1)), tile_expert=i32(tile_expert),
                lo_row=row(lo), hi_row=row(lo + ca))


def _rope_tables(length):
    rows = length // GRID_W
    r = jnp.broadcast_to(jnp.arange(rows, dtype=F32)[:, None], (rows, GRID_W)).reshape(-1)
    col = jnp.broadcast_to(jnp.arange(GRID_W, dtype=F32)[None, :], (rows, GRID_W)).reshape(-1)
    n_freq = RET_QK_DIM // 4
    inv = ROPE_BASE ** (-jnp.arange(n_freq, dtype=F32) / n_freq)
    ang = jnp.concatenate([r[:, None] * inv, col[:, None] * inv], axis=-1)
    cos = jnp.repeat(jnp.cos(ang), 2, axis=-1)
    sin = jnp.repeat(jnp.sin(ang), 2, axis=-1)
    sign = jnp.where(jnp.arange(RET_QK_DIM) % 2 == 0, -1.0, 1.0).astype(F32)
    return cos, sin * sign


def _pack_pairs(s):
    lead = s.shape[:-3]
    h, n, p = s.shape[-3:]
    s = s.reshape(lead + (h // 2, 2, n, p))
    s = jnp.moveaxis(s, -3, -2)
    return s.reshape(lead + (h // 2, n, 2 * p))


def _unpack_pairs(s):
    lead = s.shape[:-3]
    hp, n, p2 = s.shape[-3:]
    s = s.reshape(lead + (hp, n, 2, p2 // 2))
    s = jnp.moveaxis(s, -2, -3)
    return s.reshape(lead + (hp * 2, n, p2 // 2))


def kernel(x_prompt, x_sample, state_ssd, state_ret, state_hgrn, c, c_ctx, mod_w, mod_b, norm_mix_w, norm_ffn_w, even_in_w, ssd_conv_w, ssd_conv_b, ssd_dt_bias, ssd_a_log, ssd_d, ssd_norm_w, ret_decay_logit, even_out_w, odd_in_w, hgrn_lower_bound, hgrn_norm_w, odd_out_w, router_w, router_b, exp_w_gate, exp_b_gate, exp_w_up, exp_b_up, exp_w_down, exp_b_down, final_norm_w):
    bp, lp, d = x_prompt.shape
    bs, ls, _ = x_sample.shape
    depth = mod_w.shape[0]
    tp, ts = bp * lp, bs * ls
    t = tp + ts
    assert lp == TOKEN_BLOCK and ls % TOKEN_BLOCK == 0 and tp % ls == 0 and d == D_MODEL
    assert 1 + bs <= COND_ROWS
    prompt_blocks = tp // TOKEN_BLOCK
    cond_map = functools.partial(_cond_row, prompt_blocks=prompt_blocks, blocks_per_request=ls // TOKEN_BLOCK)

    x = jnp.concatenate([x_prompt.reshape(tp, d), x_sample.reshape(ts, d)], axis=0)
    cond = jnp.zeros((COND_ROWS, d), F32).at[0].set(c_ctx).at[1:1 + bs].set(c)
    mod = _modulation(cond, mod_w, mod_b)
    mod = mod.reshape(depth, COND_ROWS, 1, 6 * d)

    rope = _rope_tables(ls)
    n_tiles = -(-(t * TOP_K + (t // TOKEN_BLOCK) * N_EXPERTS * (ROW_ALIGN - 1)
                  + N_EXPERTS * (EXPERT_TILE - 1)) // EXPERT_TILE)

    lb = jnp.cumsum(jax.nn.softmax(hgrn_lower_bound.astype(F32), axis=0), axis=0)
    lb = lb - lb[0]
    log_lb = jnp.log(lb)
    log1m_lb = jnp.log1p(-lb)

    new_ssd, new_ret, new_hgrn = [], [], []
    for layer in range(depth):
        j = layer // 2
        mod_l = mod[layer]
        nw_mix = norm_mix_w[layer].reshape(1, d)
        if layer % 2 == 0:
            w = even_in_w[j]
            o = 0
            cols = {}
            for name, n in (("z", SSD_INNER), ("xbc", SSD_CONV_CH), ("dt", 2 * SSD_HEADS), ("q", RET_QK),
                            ("k", RET_QK), ("v", RET_V), ("g", RET_V)):
                cols[name] = w[:, o:o + n]
                o += n
            dt_pad = jnp.pad(cols["dt"], ((0, 0), (0, LANES - 2 * SSD_HEADS)))
            w_r = jnp.concatenate([cols["z"], cols["xbc"], cols["q"], cols["k"], cols["v"], cols["g"], dt_pad],
                                  axis=1).astype(BF16)
            z, xbc, q, k, v, g, dt_raw = _inproj(
                x, mod_l, nw_mix, w_r,
                (SSD_INNER, SSD_CONV_CH, RET_QK, RET_QK, RET_V, RET_V, LANES),
                (F32, F32, F32, F32, BF16, F32, F32), cond_map)
            pad32 = lambda a: jnp.pad(a.reshape(1, -1).astype(F32), ((0, 0), (0, LANES - 2 * SSD_HEADS)))
            ssd_args = (xbc, dt_raw, ssd_conv_w[j], ssd_conv_b[j].reshape(1, -1), pad32(ssd_dt_bias[j]),
                        pad32(ssd_a_log[j]), jnp.repeat(ssd_d[j], SSD_HEAD_DIM).reshape(1, -1))
            y_ssd, st_ssd = _ssd_scan(*ssd_args, row0=0, n_seq=bp, seq_len=lp, h0=None,
                                      prev_out=jnp.zeros((t, SSD_INNER), F32))
            h0_ssd = _pack_pairs(state_ssd[:, j])
            (y_ssd,) = _ssd_scan(*ssd_args, row0=tp, n_seq=bs, seq_len=ls, h0=h0_ssd, prev_out=y_ssd)
            new_ssd.append(_unpack_pairs(st_ssd))
            lg_rows = jnp.broadcast_to(ret_decay_logit[j].reshape(2 * RET_HEADS, 1).astype(F32),
                                       (2 * RET_HEADS, LANES))
            y_ret, st_ret = _ret_scan(q, k, v, g, lg_rows, None, row0=0, n_seq=bp, seq_len=lp, h0=None,
                                      prev_out=jnp.zeros((t, RET_V), BF16), layer_j=j)
            (y_ret,) = _ret_scan(q, k, v, g, lg_rows, rope, row0=tp, n_seq=bs, seq_len=ls, h0=state_ret,
                                 prev_out=y_ret, layer_j=j)
            new_ret.append(st_ret)
            x = _even_out(y_ssd, z, ssd_norm_w[j].reshape(1, -1), y_ret, even_out_w[j].astype(BF16), x, mod_l,
                          cond_map)
        else:
            w_r = odd_in_w[j].astype(BF16)
            n = HGRN_HEADS * HGRN_HEAD_DIM
            q, ff, fb, iv, g = _inproj(x, mod_l, nw_mix, w_r, (n,) * 5, (F32, F32, F32, BF16, F32), cond_map)
            hargs = (q, ff, fb, iv, g, log_lb[j], log1m_lb[j], hgrn_norm_w[j].reshape(1, -1))
            o_h, st_h = _hgrn_scan(*hargs, row0=0, n_seq=bp, seq_len=lp, h0=None,
                                   prev_out=jnp.zeros((t, n), BF16), layer_j=j)
            (o_h,) = _hgrn_scan(*hargs, row0=tp, n_seq=bs, seq_len=ls, h0=state_hgrn, prev_out=o_h, layer_j=j)
            new_hgrn.append(st_h)
            x = _odd_out(o_h, odd_out_w[j].astype(BF16), x, mod_l, cond_map)

        rw_pad = jnp.pad(router_w[layer], ((0, 0), (0, LANES - N_EXPERTS)))
        rb_pad = jnp.pad(router_b[layer].reshape(1, -1), ((0, 0), (0, LANES - N_EXPERTS)), constant_values=NEG_BIG)
        hffn, gates, rank, cnt_tiles = _router(x, mod_l, norm_ffn_w[layer].reshape(1, d), rw_pad, rb_pad, cond_map)
        tables = _routing_tables(cnt_tiles, n_tiles)
        x_sorted = _dispatch(hffn, rank, tables, n_tiles * EXPERT_TILE)
        e, dff = exp_b_gate.shape[1], exp_b_gate.shape[2]
        y_sorted = _experts(x_sorted, tables["tile_expert"], tables["n_used"], exp_w_gate, exp_w_up,
                            exp_w_down, exp_b_gate.reshape(depth, e, 1, dff), exp_b_up.reshape(depth, e, 1, dff),
                            exp_b_down.reshape(depth, e, 1, d), layer)
        x = _combine(y_sorted, rank, gates, tables, x, mod_l, final_norm_w.reshape(1, d), cond_map,
                     final_norm=(layer == depth - 1))

    y_prompt = x[:tp].reshape(bp, lp, d)
    y_sample = x[tp:].reshape(bs, ls, d)
    return (y_prompt, y_sample, jnp.stack(new_ssd, axis=1), jnp.stack(new_ret, axis=1),
            jnp.stack(new_hgrn, axis=1))
```

```python
import functools
import math

import jax
import jax.numpy as jnp
from jax import lax
from jax.experimental import pallas as pl
from jax.experimental.pallas import tpu as pltpu

F32 = jnp.float32
BF16 = jnp.bfloat16
HIGHEST = lax.Precision.HIGHEST

D_MODEL = 1024
GRID_W = 64
SSD_HEADS = 16
SSD_HEAD_DIM = 64
SSD_INNER = SSD_HEADS * SSD_HEAD_DIM
SSD_STATE = 128
SSD_GROUPS = 2
SSD_CONV_CH = SSD_INNER + 2 * SSD_GROUPS * SSD_STATE
RET_HEADS = 4
RET_QK_DIM = 128
RET_V_DIM = 256
RET_QK = RET_HEADS * RET_QK_DIM
RET_V = RET_HEADS * RET_V_DIM
ROPE_BASE = 10000.0
HGRN_HEADS = 8
HGRN_HEAD_DIM = 128
N_EXPERTS = 32
TOP_K = 4
SWIGLU_LIMIT = 7.0
SWIGLU_ALPHA = 1.702
NORM_EPS = 1e-6

LANES = 128
SUBLANES = 8
TOKEN_BLOCK = 256
SCAN_CHUNK = 256
EXPERT_TILE = 256
ROW_ALIGN = 16
PACK_ROWS = -(-(TOKEN_BLOCK * TOP_K + N_EXPERTS * (ROW_ALIGN - 1)) // 256) * 256
GROUP_CHUNK_BITS = (TOKEN_BLOCK // ROW_ALIGN).bit_length()
TAIL_CHUNK_BITS = (EXPERT_TILE // ROW_ALIGN - 1).bit_length()
COND_ROWS = 16
NEG_BIG = -1e30
LOG2E = math.log2(math.e)
VMEM_LIMIT = 56 * 1024 * 1024


def _cparams(n_axes):
    return pltpu.CompilerParams(dimension_semantics=("arbitrary",) * n_axes,
                                vmem_limit_bytes=VMEM_LIMIT)


def _sigmoid(x):
    return 1.0 / (1.0 + jnp.exp(-x))


def _silu(x):
    return x * _sigmoid(x)


def _softplus(x):
    return jnp.maximum(x, 0.0) + jnp.log(1.0 + jnp.exp(-jnp.abs(x)))


def _log_sigmoid(x):
    return jnp.minimum(x, 0.0) - jnp.log(1.0 + jnp.exp(-jnp.abs(x)))


def _dot(a, b):
    return jnp.dot(a, b, preferred_element_type=F32)


def _dot_nt(a, b):
    return lax.dot_general(a, b, (((1,), (1,)), ((), ())), preferred_element_type=F32)


def _dot_tn(a, b):
    return lax.dot_general(a, b, (((0,), (0,)), ((), ())), preferred_element_type=F32)


def _cond_row(i, prompt_blocks, blocks_per_request):
    return jnp.where(i < prompt_blocks, 0, 1 + (i - prompt_blocks) // blocks_per_request)


def _mod_kernel(cond_ref, w_ref, b_ref, o_ref):
    sc = _silu(cond_ref[...]).astype(BF16)
    o_ref[...] = _dot(sc, w_ref[...].astype(BF16)) + b_ref[...]


def _modulation(cond, mod_w, mod_b):
    depth, d, n = mod_w.shape
    tn = 1536
    return pl.pallas_call(
        _mod_kernel,
        grid=(depth, n // tn),
        in_specs=[pl.BlockSpec((COND_ROWS, d), lambda l, j: (0, 0)),
                  pl.BlockSpec((None, d, tn), lambda l, j: (l, 0, j)),
                  pl.BlockSpec((None, 1, tn), lambda l, j: (l, 0, j))],
        out_specs=pl.BlockSpec((None, COND_ROWS, tn), lambda l, j: (l, 0, j)),
        out_shape=jax.ShapeDtypeStruct((depth, COND_ROWS, n), F32),
        compiler_params=_cparams(2),
        name="modulation",
    )(cond, mod_w, mod_b.reshape(depth, 1, n))


def _norm_mod(x, nw, mod, shift_idx, scale_idx):
    d = x.shape[-1]
    xn = x * lax.rsqrt(jnp.mean(x * x, axis=-1, keepdims=True) + NORM_EPS) * nw
    sh = mod[:, shift_idx * d:(shift_idx + 1) * d]
    sc = mod[:, scale_idx * d:(scale_idx + 1) * d]
    return xn * (1.0 + sc) + sh


def _inproj_kernel(x_ref, mod_ref, nw_ref, w_ref, *out_refs, splits):
    h = _norm_mod(x_ref[...], nw_ref[...], mod_ref[...], 0, 1).astype(BF16)
    off = 0
    for o_ref, n in zip(out_refs, splits):
        o_ref[...] = _dot(h, w_ref[:, off:off + n]).astype(o_ref.dtype)
        off += n


def _inproj(x, mod_l, nw, w, splits, dtypes, cond_map):
    t, d = x.shape
    n = w.shape[1]
    assert sum(splits) == n
    nb = t // TOKEN_BLOCK
    return pl.pallas_call(
        functools.partial(_inproj_kernel, splits=tuple(splits)),
        grid=(nb,),
        in_specs=[pl.BlockSpec((TOKEN_BLOCK, d), lambda i: (i, 0)),
                  pl.BlockSpec((None, 1, mod_l.shape[-1]), lambda i: (cond_map(i), 0, 0)),
                  pl.BlockSpec((1, d), lambda i: (0, 0)),
                  pl.BlockSpec((d, n), lambda i: (0, 0))],
        out_specs=[pl.BlockSpec((TOKEN_BLOCK, s), lambda i: (i, 0)) for s in splits],
        out_shape=[jax.ShapeDtypeStruct((t, s), dt) for s, dt in zip(splits, dtypes)],
        compiler_params=_cparams(1),
        name="inproj",
    )(x, mod_l, nw, w)


def _even_out_kernel(yssd_ref, z_ref, nw_ref, yret_ref, w_ref, x_ref, mod_ref, o_ref):
    d = x_ref.shape[-1]
    u = yssd_ref[...] * _silu(z_ref[...])
    un = u * lax.rsqrt(jnp.mean(u * u, axis=-1, keepdims=True) + NORM_EPS) * nw_ref[...]
    k1 = yssd_ref.shape[-1]
    acc = _dot(un.astype(BF16), w_ref[0:k1, :]) + _dot(yret_ref[...], w_ref[k1:, :])
    o_ref[...] = x_ref[...] + mod_ref[:, 2 * d:3 * d] * acc


def _even_out(yssd, z, nw, yret, w, x, mod_l, cond_map):
    t, d = x.shape
    nb = t // TOKEN_BLOCK
    row = lambda i: (i, 0)
    const = lambda i: (0, 0)
    return pl.pallas_call(
        _even_out_kernel,
        grid=(nb,),
        in_specs=[pl.BlockSpec((TOKEN_BLOCK, yssd.shape[1]), row),
                  pl.BlockSpec((TOKEN_BLOCK, z.shape[1]), row),
                  pl.BlockSpec((1, nw.shape[1]), const),
                  pl.BlockSpec((TOKEN_BLOCK, yret.shape[1]), row),
                  pl.BlockSpec(w.shape, const),
                  pl.BlockSpec((TOKEN_BLOCK, d), row),
                  pl.BlockSpec((None, 1, mod_l.shape[-1]), lambda i: (cond_map(i), 0, 0))],
        out_specs=pl.BlockSpec((TOKEN_BLOCK, d), row),
        out_shape=jax.ShapeDtypeStruct((t, d), F32),
        compiler_params=_cparams(1),
        name="even_out",
    )(yssd, z, nw, yret, w, x, mod_l)


def _odd_out_kernel(a_ref, w_ref, x_ref, mod_ref, o_ref):
    d = x_ref.shape[-1]
    o_ref[...] = x_ref[...] + mod_ref[:, 2 * d:3 * d] * _dot(a_ref[...], w_ref[...])


def _odd_out(a, w, x, mod_l, cond_map):
    t, d = x.shape
    nb = t // TOKEN_BLOCK
    row = lambda i: (i, 0)
    return pl.pallas_call(
        _odd_out_kernel,
        grid=(nb,),
        in_specs=[pl.BlockSpec((TOKEN_BLOCK, a.shape[1]), row),
                  pl.BlockSpec(w.shape, lambda i: (0, 0)),
                  pl.BlockSpec((TOKEN_BLOCK, d), row),
                  pl.BlockSpec((None, 1, mod_l.shape[-1]), lambda i: (cond_map(i), 0, 0))],
        out_specs=pl.BlockSpec((TOKEN_BLOCK, d), row),
        out_shape=jax.ShapeDtypeStruct((t, d), F32),
        compiler_params=_cparams(1),
        name="odd_out",
    )(a, w, x, mod_l)


def _tri(q, upper):
    r = lax.broadcasted_iota(jnp.int32, (q, q), 0)
    c = lax.broadcasted_iota(jnp.int32, (q, q), 1)
    return (c >= r) if upper else (c <= r)


def _prefix_suffix_sums(x):
    q = x.shape[0]
    m = _tri(q, upper=False).astype(BF16)
    pre = sum(_dot(m, part) for part in _split3(x))
    return pre, pre[q - 1:q, :] - pre + x


def _ssd_kernel(*refs, seq_len, has_h0, emit_state):
    it = iter(refs)
    x_ref, b_ref, c_ref = next(it), next(it), next(it)
    cwx_ref, cwb_ref, cwc_ref = next(it), next(it), next(it)
    cbx_ref, cbb_ref, cbc_ref = next(it), next(it), next(it)
    dt_ref, dtb_ref, alog_ref, dskip_ref = next(it), next(it), next(it), next(it)
    h0_ref = next(it) if has_h0 else None
    next(it)
    y_ref = next(it)
    st_ref = next(it) if emit_state else None

    L = seq_len
    q = SCAN_CHUNK
    nc = L // q
    pair = pl.program_id(1)
    half = SSD_HEAD_DIM

    row = lax.broadcasted_iota(jnp.int32, (L, LANES), 0)

    def conv_silu(ref, w_ref, bias_ref):
        x = ref[...]
        w = w_ref[...]
        prev = jnp.where(row == 0, 0.0, pltpu.roll(x, 1, axis=0))
        nxt = jnp.where(row == L - 1, 0.0, pltpu.roll(x, L - 1, axis=0))
        return _silu(prev * w[0:1, :] + x * w[1:2, :] + nxt * w[2:3, :] + bias_ref[...])

    xs = conv_silu(x_ref, cwx_ref, cbx_ref)
    bm = conv_silu(b_ref, cwb_ref, cbb_ref)
    cm = conv_silu(c_ref, cwc_ref, cbc_ref)
    xs_bf, bm_bf, cm_bf = xs.astype(BF16), bm.astype(BF16), cm.astype(BF16)

    dt = _softplus(dt_ref[...] + dtb_ref[...])
    la = -dt * jnp.exp(alog_ref[...])

    lane = lax.broadcasted_iota(jnp.int32, (q, LANES), 1)
    lane_row = lax.broadcasted_iota(jnp.int32, (1, LANES), 1)
    first = lane < half
    tril = _tri(q, upper=False)
    triu = _tri(q, upper=True)

    def pick_col(arr, col):
        return jnp.sum(jnp.where(lane == col, arr, 0.0), axis=1, keepdims=True)

    def pick_row(arr_t, r):
        rr = lax.broadcasted_iota(jnp.int32, arr_t.shape, 0)
        return jnp.sum(jnp.where(rr == r, arr_t, 0.0), axis=0, keepdims=True)

    y_parts, e_f, e_b, d_sf, d_sb, dec_f, dec_b = [], [], [], [], [], [], []
    for c in range(nc):
        sl = slice(c * q, (c + 1) * q)
        la_c, dt_c = la[sl], dt[sl]
        cum, rcs = _prefix_suffix_sums(la_c)
        cum_t, rcs_t, dt_t = cum.T, rcs.T, dt_c.T
        s_g = _dot_nt(cm_bf[sl], bm_bf[sl])
        y_c = None
        cols = {}
        for hh in range(2):
            head = 2 * pair + hh
            cf, cb = pick_col(cum, head), pick_col(rcs, SSD_HEADS + head)
            rf, rb = pick_row(cum_t, head), pick_row(rcs_t, SSD_HEADS + head)
            dtf_r, dtb_r = pick_row(dt_t, head), pick_row(dt_t, SSD_HEADS + head)
            dec = (jnp.exp(jnp.where(tril, cf - rf, NEG_BIG)) * dtf_r
                   + jnp.exp(jnp.where(triu, cb - rb, NEG_BIG)) * dtb_r)
            y_h = _dot((s_g * dec).astype(BF16), xs_bf[sl])
            y_c = y_h if hh == 0 else jnp.where(first, y_c, y_h)
            cols[hh] = (cf, cb, pick_col(dt_c, head), pick_col(dt_c, SSD_HEADS + head))
        cum_e = jnp.where(first, cols[0][0], cols[1][0])
        rcs_e = jnp.where(first, cols[0][1], cols[1][1])
        dtf_e = jnp.where(first, cols[0][2], cols[1][2])
        dtb_e = jnp.where(first, cols[0][3], cols[1][3])
        y_parts.append(y_c)
        e_f.append(jnp.exp(cum_e))
        e_b.append(jnp.exp(rcs_e))
        xs_c = xs[sl]
        wf = xs_c * dtf_e * jnp.exp(cum_e[q - 1:q, :] - cum_e)
        wb = xs_c * dtb_e * jnp.exp(rcs_e[0:1, :] - rcs_e)
        d_sf.append(_dot_tn(bm_bf[sl], wf.astype(BF16)))
        d_sb.append(_dot_tn(bm_bf[sl], wb.astype(BF16)))
        dec_f.append(jnp.exp(cum_e[q - 1:q, :]))
        dec_b.append(jnp.exp(rcs_e[0:1, :]))

    zero = jnp.zeros((SSD_STATE, LANES), F32)
    sf = h0_ref[0] if has_h0 else zero
    sf_in = []
    for c in range(nc):
        sf_in.append(sf)
        sf = dec_f[c] * sf + d_sf[c]
    sb = h0_ref[1] if has_h0 else zero
    sb_in = [None] * nc
    for c in reversed(range(nc)):
        sb_in[c] = sb
        sb = dec_b[c] * sb + d_sb[c]

    dsk = dskip_ref[...]
    for c in range(nc):
        sl = slice(c * q, (c + 1) * q)
        y_c = y_parts[c] + xs[sl] * dsk
        if has_h0 or c > 0:
            y_c = y_c + e_f[c] * _dot(cm_bf[sl], sf_in[c].astype(BF16))
        if has_h0 or c < nc - 1:
            y_c = y_c + e_b[c] * _dot(cm_bf[sl], sb_in[c].astype(BF16))
        y_ref[sl, :] = y_c
    if emit_state:
        st_ref[0] = sf
        st_ref[1] = sb
    del lane_row


def _ssd_scan(xbc, dt_raw, conv_w, conv_b, dt_bias, a_log, d_skip, *, row0, n_seq, seq_len,
              h0, prev_out):
    t = xbc.shape[0]
    blk0 = row0 // seq_len
    n_pairs = SSD_HEADS // 2
    pairs_per_group = n_pairs // SSD_GROUPS
    xoff, boff, coff = 0, SSD_INNER // LANES, (SSD_INNER + SSD_GROUPS * SSD_STATE) // LANES
    has_h0 = h0 is not None
    emit_state = not has_h0

    def seq_blk(col_fn):
        return lambda s, p: (blk0 + s, col_fn(p))

    fx = lambda p: xoff + p
    fb = lambda p: boff + p // pairs_per_group
    fc = lambda p: coff + p // pairs_per_group
    vec = lambda rows, col_fn: pl.BlockSpec((rows, LANES), lambda s, p: (0, col_fn(p)))
    in_specs = [pl.BlockSpec((seq_len, LANES), seq_blk(fx)),
                pl.BlockSpec((seq_len, LANES), seq_blk(fb)),
                pl.BlockSpec((seq_len, LANES), seq_blk(fc)),
                vec(3, fx), vec(3, fb), vec(3, fc),
                vec(1, fx), vec(1, fb), vec(1, fc),
                pl.BlockSpec((seq_len, LANES), lambda s, p: (blk0 + s, 0)),
                pl.BlockSpec((1, LANES), lambda s, p: (0, 0)),
                pl.BlockSpec((1, LANES), lambda s, p: (0, 0)),
                vec(1, fx)]
    args = [xbc, xbc, xbc, conv_w, conv_w, conv_w, conv_b, conv_b, conv_b,
            dt_raw, dt_bias, a_log, d_skip]
    io_alias = {}
    if has_h0:
        in_specs.append(pl.BlockSpec((None, 2, None, SSD_STATE, LANES), lambda s, p: (s, 0, p, 0, 0)))
        args.append(h0)
    io_alias = {len(args): 0}
    in_specs.append(pl.BlockSpec(memory_space=pl.ANY))
    args.append(prev_out)
    out_specs = [pl.BlockSpec((seq_len, LANES), lambda s, p: (blk0 + s, p))]
    out_shape = [jax.ShapeDtypeStruct((t, SSD_INNER), F32)]
    if emit_state:
        out_specs.append(pl.BlockSpec((None, 2, None, SSD_STATE, LANES), lambda s, p: (s, 0, p, 0, 0)))
        out_shape.append(jax.ShapeDtypeStruct((n_seq, 2, n_pairs, SSD_STATE, LANES), F32))
    kern = functools.partial(_ssd_kernel, seq_len=seq_len, has_h0=has_h0, emit_state=emit_state)
    return pl.pallas_call(
        kern, grid=(n_seq, n_pairs), in_specs=in_specs, out_specs=out_specs, out_shape=out_shape,
        input_output_aliases=io_alias, compiler_params=_cparams(2), name="ssd_scan",
    )(*args)


def _ret_kernel(*refs, seq_len, has_h0, emit_state, use_rope):
    it = iter(refs)
    q_ref, k_ref, v_ref, g_ref, lg_ref = next(it), next(it), next(it), next(it), next(it)
    cos_ref = next(it) if use_rope else None
    sin_ref = next(it) if use_rope else None
    h0_ref = next(it) if has_h0 else None
    next(it)
    o_ref = next(it)
    st_ref = next(it) if emit_state else None

    L = seq_len
    qn = SCAN_CHUNK
    nc = L // qn
    head = pl.program_id(1)

    qf = q_ref[...]
    kf = k_ref[...] * (RET_QK_DIM ** -0.5)
    if use_rope:
        lane = lax.broadcasted_iota(jnp.int32, (L, LANES), 1)
        even = (lane % 2) == 0
        cos, sin = cos_ref[...], sin_ref[...]

        def rope(x):
            swapped = jnp.where(even, pltpu.roll(x, LANES - 1, axis=1), pltpu.roll(x, 1, axis=1))
            return x * cos + swapped * sin
        qf, kf = rope(qf), rope(kf)
    q_bf, k_bf = qf.astype(BF16), kf.astype(BF16)
    v_bf = v_ref[...]

    lg = _log_sigmoid(lg_ref[...])
    rr = lax.broadcasted_iota(jnp.int32, lg.shape, 0)
    lgf = jnp.sum(jnp.where(rr == head, lg, 0.0), axis=0, keepdims=True)[:, 0:1]
    lgb = jnp.sum(jnp.where(rr == RET_HEADS + head, lg, 0.0), axis=0, keepdims=True)[:, 0:1]

    ti = lax.broadcasted_iota(jnp.int32, (qn, qn), 0)
    si = lax.broadcasted_iota(jnp.int32, (qn, qn), 1)
    dist = (ti - si).astype(F32)
    dec = (jnp.exp(jnp.where(ti >= si, dist * lgf, NEG_BIG))
           + jnp.exp(jnp.where(si >= ti, -dist * lgb, NEG_BIG)))
    tcol = lax.broadcasted_iota(jnp.int32, (qn, 1), 0).astype(F32)
    ef = jnp.exp((tcol + 1.0) * lgf)
    eb = jnp.exp((qn - tcol) * lgb)
    wf = jnp.exp((qn - 1.0 - tcol) * lgf)
    wb = jnp.exp(tcol * lgb)
    dec_f = jnp.exp(qn * lgf)
    dec_b = jnp.exp(qn * lgb)

    y_parts, d_sf, d_sb = [], [], []
    for c in range(nc):
        sl = slice(c * qn, (c + 1) * qn)
        s = _dot_nt(q_bf[sl], k_bf[sl])
        y_parts.append(_dot((s * dec).astype(BF16), v_bf[sl]))
        d_sf.append(_dot_tn((kf[sl] * wf).astype(BF16), v_bf[sl]))
        d_sb.append(_dot_tn((kf[sl] * wb).astype(BF16), v_bf[sl]))

    zero = jnp.zeros((RET_QK_DIM, RET_V_DIM), F32)
    sf = h0_ref[0] if has_h0 else zero
    sf_in = []
    for c in range(nc):
        sf_in.append(sf)
        sf = dec_f * sf + d_sf[c]
    sb = h0_ref[1] if has_h0 else zero
    sb_in = [None] * nc
    for c in reversed(range(nc)):
        sb_in[c] = sb
        sb = dec_b * sb + d_sb[c]

    for c in range(nc):
        sl = slice(c * qn, (c + 1) * qn)
        o = y_parts[c]
        if has_h0 or c > 0:
            o = o + ef * _dot(q_bf[sl], sf_in[c].astype(BF16))
        if has_h0 or c < nc - 1:
            o = o + eb * _dot(q_bf[sl], sb_in[c].astype(BF16))
        mu = jnp.mean(o, axis=-1, keepdims=True)
        var = jnp.mean(jnp.square(o - mu), axis=-1, keepdims=True)
        on = (o - mu) * lax.rsqrt(var + NORM_EPS)
        o_ref[sl, :] = (on * _silu(g_ref[sl, :])).astype(o_ref.dtype)
    if emit_state:
        st_ref[0] = sf
        st_ref[1] = sb


def _ret_scan(q, k, v, g, lg_rows, rope, *, row0, n_seq, seq_len, h0, prev_out, layer_j):
    t = q.shape[0]
    blk0 = row0 // seq_len
    has_h0 = h0 is not None
    emit_state = not has_h0
    use_rope = rope is not None
    blk = lambda w: pl.BlockSpec((seq_len, w), lambda s, h: (blk0 + s, h))
    in_specs = [blk(RET_QK_DIM), blk(RET_QK_DIM), blk(RET_V_DIM), blk(RET_V_DIM),
                pl.BlockSpec(lg_rows.shape, lambda s, h: (0, 0))]
    args = [q, k, v, g, lg_rows]
    if use_rope:
        in_specs += [pl.BlockSpec((seq_len, LANES), lambda s, h: (0, 0))] * 2
        args += [rope[0], rope[1]]
    if has_h0:
        in_specs.append(pl.BlockSpec((None, None, 2, None, RET_QK_DIM, RET_V_DIM),
                                     lambda s, h: (s, layer_j, 0, h, 0, 0)))
        args.append(h0)
    io_alias = {}
    io_alias = {len(args): 0}
    in_specs.append(pl.BlockSpec(memory_space=pl.ANY))
    args.append(prev_out)
    out_specs = [pl.BlockSpec((seq_len, RET_V_DIM), lambda s, h: (blk0 + s, h))]
    out_shape = [jax.ShapeDtypeStruct((t, RET_V), BF16)]
    if emit_state:
        out_specs.append(pl.BlockSpec((None, 2, None, RET_QK_DIM, RET_V_DIM), lambda s, h: (s, 0, h, 0, 0)))
        out_shape.append(jax.ShapeDtypeStruct((n_seq, 2, RET_HEADS, RET_QK_DIM, RET_V_DIM), F32))
    kern = functools.partial(_ret_kernel, seq_len=seq_len, has_h0=has_h0, emit_state=emit_state,
                             use_rope=use_rope)
    return pl.pallas_call(
        kern, grid=(n_seq, RET_HEADS), in_specs=in_specs, out_specs=out_specs, out_shape=out_shape,
        input_output_aliases=io_alias, compiler_params=_cparams(2), name="ret_scan",
    )(*args)


def _pair_boundaries(block_last, block_first, upper, h):
    qn = block_last.shape[0]
    last_prev = pltpu.roll(block_last, h, axis=0)
    last_next = pltpu.roll(block_last, qn - h, axis=0)
    first_prev = pltpu.roll(block_first, h, axis=0)
    first_next = pltpu.roll(block_first, qn - h, axis=0)
    ref_f = jnp.where(upper, last_prev, block_last)
    ref_b = jnp.where(upper, block_first, first_next)
    return ref_f, ref_b, jnp.where(upper, block_last, last_next), jnp.where(upper, first_prev, block_first)


def _hgrn_kernel(*refs, seq_len, has_h0, emit_state):
    it = iter(refs)
    q_ref, ff_ref, fb_ref, i_ref, g_ref = next(it), next(it), next(it), next(it), next(it)
    llb_ref, l1m_ref, nw_ref = next(it), next(it), next(it)
    h0_ref = next(it) if has_h0 else None
    next(it)
    o_ref = next(it)
    st_ref = next(it) if emit_state else None

    L = seq_len
    qn = SCAN_CHUNK
    nc = L // qn

    def log_forget(fz, d):
        a = llb_ref[d:d + 1, :]
        b = l1m_ref[d:d + 1, :] + _log_sigmoid(fz)
        m = jnp.maximum(a, b)
        return m + jnp.log(1.0 + jnp.exp(-jnp.abs(a - b)))

    lf_f = log_forget(ff_ref[...], 0)
    lf_b = log_forget(fb_ref[...], 1)
    k_f = 1.0 - jnp.exp(lf_f)
    k_b = 1.0 - jnp.exp(lf_b)
    qv = q_ref[...]
    v_bf = i_ref[...]

    t_col = lax.broadcasted_iota(jnp.int32, (qn, 1), 0)
    ti = lax.broadcasted_iota(jnp.int32, (qn, qn), 0)
    si = lax.broadcasted_iota(jnp.int32, (qn, qn), 1)
    pair_level = 31 - lax.clz(ti ^ si)
    n = HGRN_HEAD_DIM

    y_parts, e_f, e_b, d_sf, d_sb, dec_f, dec_b = [], [], [], [], [], [], []
    for c in range(nc):
        sl = slice(c * qn, (c + 1) * qn)
        q_c, kf_c, kb_c = qv[sl], k_f[sl], k_b[sl]
        pre, suf = _prefix_suffix_sums(jnp.concatenate([lf_f[sl], lf_b[sl]], axis=1))
        cum, rcs = pre[:, :n], suf[:, n:]
        scores = jnp.zeros((qn, qn), F32)
        cum2, rcs2 = cum * LOG2E, rcs * LOG2E
        block_last, block_first = cum2, rcs2
        h = 1
        while h < qn:
            upper = ((t_col // h) % 2) == 1
            ref_f, ref_b, block_last, block_first = _pair_boundaries(block_last, block_first, upper, h)
            d_f, d_b = cum2 - ref_f, rcs2 - ref_b
            aq = jnp.exp2(jnp.where(upper, d_f, d_b))
            ak = jnp.exp2(-jnp.where(upper, d_b, d_f))
            qt = (q_c * aq).astype(BF16)
            kt = (jnp.where(upper, kb_c, kf_c) * ak).astype(BF16)
            s_l = _dot_nt(qt, kt)
            scores = jnp.where(pair_level == h.bit_length() - 1, s_l, scores)
            h *= 2
        diag = jnp.sum(q_c * (kf_c + kb_c), axis=-1, keepdims=True)
        y_parts.append(_dot(scores.astype(BF16), v_bf[sl]) + diag * v_bf[sl].astype(F32))
        e_f.append(jnp.exp(cum))
        e_b.append(jnp.exp(rcs))
        d_sf.append(_dot_tn((kf_c * jnp.exp(cum[qn - 1:qn, :] - cum)).astype(BF16), v_bf[sl]))
        d_sb.append(_dot_tn((kb_c * jnp.exp(rcs[0:1, :] - rcs)).astype(BF16), v_bf[sl]))
        dec_f.append(jnp.broadcast_to(jnp.exp(cum[qn - 1:qn, :]), (n, n)).T)
        dec_b.append(jnp.broadcast_to(jnp.exp(rcs[0:1, :]), (n, n)).T)

    zero = jnp.zeros((HGRN_HEAD_DIM, HGRN_HEAD_DIM), F32)
    sf = h0_ref[0] if has_h0 else zero
    sf_in = []
    for c in range(nc):
        sf_in.append(sf)
        sf = dec_f[c] * sf + d_sf[c]
    sb = h0_ref[1] if has_h0 else zero
    sb_in = [None] * nc
    for c in reversed(range(nc)):
        sb_in[c] = sb
        sb = dec_b[c] * sb + d_sb[c]

    nw = nw_ref[...]
    for c in range(nc):
        sl = slice(c * qn, (c + 1) * qn)
        o = y_parts[c]
        if has_h0 or c > 0:
            o = o + _dot((qv[sl] * e_f[c]).astype(BF16), sf_in[c].astype(BF16))
        if has_h0 or c < nc - 1:
            o = o + _dot((qv[sl] * e_b[c]).astype(BF16), sb_in[c].astype(BF16))
        on = o * lax.rsqrt(jnp.mean(o * o, axis=-1, keepdims=True) + NORM_EPS) * nw
        o_ref[sl, :] = (on * _silu(g_ref[sl, :])).astype(o_ref.dtype)
    if emit_state:
        st_ref[0] = sf
        st_ref[1] = sb


def _hgrn_scan(q, ff, fb, iv, g, log_lb, log1m_lb, nw, *, row0, n_seq, seq_len, h0, prev_out, layer_j):
    t = q.shape[0]
    blk0 = row0 // seq_len
    has_h0 = h0 is not None
    emit_state = not has_h0
    n = HGRN_HEAD_DIM
    blk = pl.BlockSpec((seq_len, n), lambda s, h: (blk0 + s, h))
    in_specs = [blk, blk, blk, blk, blk,
                pl.BlockSpec((2, n), lambda s, h: (0, h)),
                pl.BlockSpec((2, n), lambda s, h: (0, h)),
                pl.BlockSpec((1, n), lambda s, h: (0, h))]
    args = [q, ff, fb, iv, g, log_lb, log1m_lb, nw]
    if has_h0:
        in_specs.append(pl.BlockSpec((None, None, 2, None, n, n), lambda s, h: (s, layer_j, 0, h, 0, 0)))
        args.append(h0)
    io_alias = {}
    io_alias = {len(args): 0}
    in_specs.append(pl.BlockSpec(memory_space=pl.ANY))
    args.append(prev_out)
    out_specs = [pl.BlockSpec((seq_len, n), lambda s, h: (blk0 + s, h))]
    out_shape = [jax.ShapeDtypeStruct((t, HGRN_HEADS * n), BF16)]
    if emit_state:
        out_specs.append(pl.BlockSpec((None, 2, None, n, n), lambda s, h: (s, 0, h, 0, 0)))
        out_shape.append(jax.ShapeDtypeStruct((n_seq, 2, HGRN_HEADS, n, n), F32))
    kern = functools.partial(_hgrn_kernel, seq_len=seq_len, has_h0=has_h0, emit_state=emit_state)
    return pl.pallas_call(
        kern, grid=(n_seq, HGRN_HEADS), in_specs=in_specs, out_specs=out_specs, out_shape=out_shape,
        input_output_aliases=io_alias, compiler_params=_cparams(2), name="hgrn_scan",
    )(*args)


def _router_kernel(x_ref, mod_ref, nw_ref, rw_ref, rb_ref, h_ref, gate_ref, rank_ref, cnt_ref):
    h = _norm_mod(x_ref[...], nw_ref[...], mod_ref[...], 3, 4)
    h_ref[...] = h.astype(BF16)
    logits = jnp.dot(h, rw_ref[...], precision=HIGHEST, preferred_element_type=F32) + rb_ref[...]
    lane = lax.broadcasted_iota(jnp.int32, logits.shape, 1).astype(F32)
    work = logits
    vals, idxs = [], []
    for _ in range(TOP_K):
        m = jnp.max(work, axis=-1, keepdims=True)
        idx = jnp.min(jnp.where(work == m, lane, float(LANES)), axis=-1, keepdims=True)
        vals.append(m)
        idxs.append(idx)
        work = jnp.where(lane == idx, -jnp.inf, work)
    es = [jnp.exp(v - vals[0]) for v in vals]
    denom = es[0] + es[1] + es[2] + es[3]
    gates = jnp.zeros(logits.shape, F32)
    member = jnp.zeros(logits.shape, F32)
    for k in range(TOP_K):
        hit = lane == idxs[k]
        gates = jnp.where(hit, es[k] / denom, gates)
        member = jnp.where(hit, 1.0, member)
    tb = logits.shape[0]
    before = _tri(tb, upper=False) & ~_tri(tb, upper=True)
    rank = _dot(before.astype(BF16), member.astype(BF16))
    gate_ref[...] = gates
    rank_ref[...] = jnp.where(member > 0.0, rank, -1.0)
    cnt_ref[...] = jnp.sum(member, axis=0, keepdims=True).astype(jnp.int32)


def _router(x, mod_l, nw, rw_pad, rb_pad, cond_map):
    t, d = x.shape
    nb = t // TOKEN_BLOCK
    row = lambda i: (i, 0)
    const = lambda i: (0, 0)
    return pl.pallas_call(
        _router_kernel,
        grid=(nb,),
        in_specs=[pl.BlockSpec((TOKEN_BLOCK, d), row),
                  pl.BlockSpec((None, 1, mod_l.shape[-1]), lambda i: (cond_map(i), 0, 0)),
                  pl.BlockSpec((1, d), const),
                  pl.BlockSpec((d, LANES), const),
                  pl.BlockSpec((1, LANES), const)],
        out_specs=[pl.BlockSpec((TOKEN_BLOCK, d), row),
                   pl.BlockSpec((TOKEN_BLOCK, LANES), row),
                   pl.BlockSpec((TOKEN_BLOCK, LANES), row),
                   pl.BlockSpec((None, 1, LANES), lambda i: (i, 0, 0))],
        out_shape=[jax.ShapeDtypeStruct((t, d), BF16),
                   jax.ShapeDtypeStruct((t, LANES), F32),
                   jax.ShapeDtypeStruct((t, LANES), F32),
                   jax.ShapeDtypeStruct((nb, 1, LANES), jnp.int32)],
        compiler_params=_cparams(1),
        name="router",
    )(x, mod_l, nw, rw_pad, rb_pad)


def _split2(x):
    hi = x.astype(BF16)
    return hi, (x - hi.astype(F32)).astype(BF16)


def _split3(x):
    hi, mid = _split2(x)
    r2 = x - hi.astype(F32) - mid.astype(F32)
    return hi, mid, r2.astype(BF16)


def _packed_owner(lo_row, hi_row):
    r = lax.broadcasted_iota(jnp.int32, (PACK_ROWS, LANES), 0).astype(F32)
    return (r >= lo_row) & (r < hi_row)


def _chunks(units, bits):
    for b in range(bits):
        yield (((units >> b) & 1) == 1,
               pl.multiple_of(((units >> (b + 1)) << (b + 1)) * ROW_ALIGN, ROW_ALIGN),
               ROW_ALIGN << b)


def _group_chunks(i, e, start_ref, lo_ref, units_ref):
    idx = i * N_EXPERTS + e
    for present, base, rows in _chunks(units_ref[idx], GROUP_CHUNK_BITS):
        yield (present, pl.multiple_of(lo_ref[idx] + base, ROW_ALIGN),
               pl.multiple_of(start_ref[idx] + base, ROW_ALIGN), rows)


def _zero_unused_rows(tail_start_ref, tail_units_ref, nu_ref, xs_hbm, zeros_buf, sem):
    n_tiles = xs_hbm.shape[0] // EXPERT_TILE
    zeros_buf[...] = jnp.zeros_like(zeros_buf)

    def tail_copy(off, rows):
        return pltpu.make_async_copy(zeros_buf.at[pl.ds(0, rows)], xs_hbm.at[pl.ds(off, rows)], sem)

    def tile_copy(n):
        return pltpu.make_async_copy(zeros_buf, xs_hbm.at[pl.ds(pl.multiple_of(n * EXPERT_TILE, EXPERT_TILE),
                                                                EXPERT_TILE)], sem)

    def each(fn):
        for e in range(N_EXPERTS):
            for present, base, rows in _chunks(tail_units_ref[e], TAIL_CHUNK_BITS):
                @pl.when(present)
                def _():
                    fn(tail_copy(pl.multiple_of(tail_start_ref[e] + base, ROW_ALIGN), rows))

        def body(n, carry):
            fn(tile_copy(n))
            return carry
        lax.fori_loop(nu_ref[0], n_tiles, body, 0)

    each(lambda c: c.start())
    each(lambda c: c.wait())


def _dispatch_kernel(start_ref, lo_ref, units_ref, tail_start_ref, tail_units_ref, nu_ref,
                     h_ref, rank_ref, lorow_ref, hirow_ref, xs_hbm, xs_buf, zeros_buf, sem):
    i = pl.program_id(0)

    @pl.when(i == 0)
    def _():
        _zero_unused_rows(tail_start_ref, tail_units_ref, nu_ref, xs_hbm, zeros_buf, sem)

    lo_row = lorow_ref[...]
    owner = _packed_owner(lo_row, hirow_ref[...])
    owned = jnp.sum(owner.astype(F32), axis=1, keepdims=True) > 0.0
    r = lax.broadcasted_iota(jnp.int32, (PACK_ROWS, 1), 0).astype(F32)
    local = r - jnp.sum(jnp.where(owner, lo_row, 0.0), axis=1, keepdims=True)
    rank_rows = _dot_nt(owner.astype(BF16), rank_ref[...].astype(BF16))
    sel = ((rank_rows == local) & owned).astype(BF16)
    xs_buf[...] = _dot(sel, h_ref[...]).astype(BF16)

    def copy(v_off, s_off, rows):
        return pltpu.make_async_copy(xs_buf.at[pl.ds(v_off, rows)], xs_hbm.at[pl.ds(s_off, rows)], sem)

    for e in range(N_EXPERTS):
        for present, v_off, s_off, rows in _group_chunks(i, e, start_ref, lo_ref, units_ref):
            @pl.when(present)
            def _():
                copy(v_off, s_off, rows).start()
    for e in range(N_EXPERTS):
        for present, v_off, s_off, rows in _group_chunks(i, e, start_ref, lo_ref, units_ref):
            @pl.when(present)
            def _():
                copy(v_off, s_off, rows).wait()


def _dispatch(hffn, rank, tables, n_rows):
    t, d = hffn.shape
    nb = t // TOKEN_BLOCK
    row = lambda i, *_: (i, 0)
    vec = pl.BlockSpec((None, 1, LANES), lambda i, *_: (i, 0, 0))
    grid_spec = pltpu.PrefetchScalarGridSpec(
        num_scalar_prefetch=6,
        grid=(nb,),
        in_specs=[pl.BlockSpec((TOKEN_BLOCK, d), row), pl.BlockSpec((TOKEN_BLOCK, LANES), row), vec, vec],
        out_specs=pl.BlockSpec(memory_space=pl.ANY),
        scratch_shapes=[pltpu.VMEM((PACK_ROWS, d), BF16), pltpu.VMEM((EXPERT_TILE, d), BF16),
                        pltpu.SemaphoreType.DMA(())],
    )
    return pl.pallas_call(
        _dispatch_kernel, grid_spec=grid_spec,
        out_shape=jax.ShapeDtypeStruct((n_rows, d), BF16),
        compiler_params=_cparams(1), name="dispatch",
    )(tables["start"], tables["lo"], tables["units"], tables["tail_start"], tables["tail_units"],
      tables["n_used"], hffn, rank, tables["lo_row"], tables["hi_row"])


def _expert_kernel(first_ref, count_ref, nu_ref, x_hbm, wg_ref, wu_ref, wd_ref, bg_ref, bu_ref, bd_ref,
                   y_hbm, xbuf, ybuf, wg_bf, wu_bf, wd_bf, sem_in, sem_out):
    e = pl.program_id(0)
    first, count = first_ref[e], count_ref[e]
    n_tiles = y_hbm.shape[0] // EXPERT_TILE

    def rows(tile):
        return pl.ds(pl.multiple_of(tile * EXPERT_TILE, EXPERT_TILE), EXPERT_TILE)

    def x_copy(j, slot):
        return pltpu.make_async_copy(x_hbm.at[rows(first + j)], xbuf.at[slot], sem_in.at[slot])

    def y_copy(tile, slot):
        return pltpu.make_async_copy(ybuf.at[slot], y_hbm.at[rows(tile)], sem_out.at[slot])

    @pl.when(count > 0)
    def _():
        x_copy(0, 0).start()

    wg_bf[...] = wg_ref[...].astype(BF16)
    wu_bf[...] = wu_ref[...].astype(BF16)
    wd_bf[...] = wd_ref[...].astype(BF16)

    def one_tile(j, carry):
        slot = j % 2

        @pl.when(j + 1 < count)
        def _():
            x_copy(j + 1, 1 - slot).start()

        x_copy(j, slot).wait()

        @pl.when(j >= 2)
        def _():
            y_copy(first + j - 2, slot).wait()

        x = xbuf[slot]
        gt = jnp.minimum(_dot(x, wg_bf[...]) + bg_ref[...], SWIGLU_LIMIT)
        up = jnp.clip(_dot(x, wu_bf[...]) + bu_ref[...], -SWIGLU_LIMIT, SWIGLU_LIMIT)
        act = gt * _sigmoid(SWIGLU_ALPHA * gt) * (up + 1.0)
        ybuf[slot] = (_dot(act.astype(BF16), wd_bf[...]) + bd_ref[...]).astype(ybuf.dtype)
        y_copy(first + j, slot).start()
        return carry

    lax.fori_loop(0, count, one_tile, 0)
    for back in (2, 1):
        @pl.when(count >= back)
        def _():
            y_copy(first + count - back, (count - back) % 2).wait()

    @pl.when(e == pl.num_programs(0) - 1)
    def _():
        ybuf[0] = jnp.zeros(ybuf.shape[1:], ybuf.dtype)

        def start(tile, carry):
            y_copy(tile, 0).start()
            return carry

        def wait(tile, carry):
            y_copy(tile, 0).wait()
            return carry
        lax.fori_loop(nu_ref[0], n_tiles, start, 0)
        lax.fori_loop(nu_ref[0], n_tiles, wait, 0)


def _experts(x_sorted, seg_first, seg_count, n_used, wg, wu, wd, bg, bu, bd, layer):
    n_rows, d = x_sorted.shape
    n_experts, dff = wg.shape[1], wg.shape[-1]
    wspec = lambda a: pl.BlockSpec((None, None) + a.shape[2:], lambda e, *_: (layer, e, 0, 0))
    anyspec = pl.BlockSpec(memory_space=pl.ANY)
    grid_spec = pltpu.PrefetchScalarGridSpec(
        num_scalar_prefetch=3,
        grid=(n_experts,),
        in_specs=[anyspec, wspec(wg), wspec(wu), wspec(wd), wspec(bg), wspec(bu), wspec(bd)],
        out_specs=anyspec,
        scratch_shapes=[pltpu.VMEM((2, EXPERT_TILE, d), BF16), pltpu.VMEM((2, EXPERT_TILE, d), BF16),
                        pltpu.VMEM((d, dff), BF16), pltpu.VMEM((d, dff), BF16), pltpu.VMEM((dff, d), BF16),
                        pltpu.SemaphoreType.DMA((2,)), pltpu.SemaphoreType.DMA((2,))],
    )
    return pl.pallas_call(
        _expert_kernel, grid_spec=grid_spec,
        out_shape=jax.ShapeDtypeStruct((n_rows, d), BF16),
        compiler_params=_cparams(1), name="experts",
    )(seg_first, seg_count, n_used, x_sorted, wg, wu, wd, bg, bu, bd)


def _combine_kernel(start_ref, lo_ref, units_ref, y_hbm, rank_ref, gate_ref, lorow_ref, hirow_ref,
                    x_ref, mod_ref, fnw_ref, o_ref, buf, sem, *, final_norm):
    i = pl.program_id(0)
    d = x_ref.shape[-1]

    @pl.when(i == 0)
    def _():
        buf[...] = jnp.zeros_like(buf)

    def copy(v_off, s_off, rows):
        return pltpu.make_async_copy(y_hbm.at[pl.ds(s_off, rows)], buf.at[pl.ds(v_off, rows)], sem)

    for e in range(N_EXPERTS):
        for present, v_off, s_off, rows in _group_chunks(i, e, start_ref, lo_ref, units_ref):
            @pl.when(present)
            def _():
                copy(v_off, s_off, rows).start()

    lo_row = lorow_ref[...]
    owner = _packed_owner(lo_row, hirow_ref[...]).astype(BF16)
    rank_cols = _dot_nt(rank_ref[...].astype(BF16), owner)
    lo8 = jnp.broadcast_to(lo_row, (SUBLANES, LANES))
    lo_cols = sum(_dot_nt(part, owner) for part in _split3(lo8))[0:1, :]
    owned = _dot_nt(jnp.ones((SUBLANES, LANES), BF16), owner)[0:1, :] > 0.0
    col = lax.broadcasted_iota(jnp.int32, (1, PACK_ROWS), 1).astype(F32)
    match = (rank_cols == col - lo_cols) & owned
    g_hi, g_lo = _split2(gate_ref[...])
    q_hi = jnp.where(match, _dot_nt(g_hi, owner), 0.0).astype(BF16)
    q_lo = jnp.where(match, _dot_nt(g_lo, owner), 0.0).astype(BF16)

    for e in range(N_EXPERTS):
        for present, v_off, s_off, rows in _group_chunks(i, e, start_ref, lo_ref, units_ref):
            @pl.when(present)
            def _():
                copy(v_off, s_off, rows).wait()
    y = buf[...]
    x = x_ref[...] + mod_ref[:, 5 * d:6 * d] * (_dot(q_hi, y) + _dot(q_lo, y))
    if final_norm:
        x = x * lax.rsqrt(jnp.mean(x * x, axis=-1, keepdims=True) + NORM_EPS) * fnw_ref[...]
    o_ref[...] = x


def _combine(y_sorted, rank, gates, tables, x, mod_l, fnw, cond_map, final_norm):
    t, d = x.shape
    nb = t // TOKEN_BLOCK
    row = lambda i, *_: (i, 0)
    vec = pl.BlockSpec((None, 1, LANES), lambda i, *_: (i, 0, 0))
    grid_spec = pltpu.PrefetchScalarGridSpec(
        num_scalar_prefetch=3,
        grid=(nb,),
        in_specs=[pl.BlockSpec(memory_space=pl.ANY),
                  pl.BlockSpec((TOKEN_BLOCK, LANES), row),
                  pl.BlockSpec((TOKEN_BLOCK, LANES), row),
                  vec, vec,
                  pl.BlockSpec((TOKEN_BLOCK, d), row),
                  pl.BlockSpec((None, 1, mod_l.shape[-1]), lambda i, *_: (cond_map(i), 0, 0)),
                  pl.BlockSpec((1, d), lambda i, *_: (0, 0))],
        out_specs=pl.BlockSpec((TOKEN_BLOCK, d), row),
        scratch_shapes=[pltpu.VMEM((PACK_ROWS, d), BF16), pltpu.SemaphoreType.DMA(())],
    )
    return pl.pallas_call(
        functools.partial(_combine_kernel, final_norm=final_norm), grid_spec=grid_spec,
        out_shape=jax.ShapeDtypeStruct((t, d), F32),
        compiler_params=_cparams(1), name="combine",
    )(tables["start"], tables["lo"], tables["units"], y_sorted, rank, gates, tables["lo_row"],
      tables["hi_row"], x, mod_l, fnw)


def _routing_tables(cnt_tiles, n_tiles):
    nb = cnt_tiles.shape[0]
    c = cnt_tiles[:, 0, :N_EXPERTS]
    ca = ((c + ROW_ALIGN - 1) // ROW_ALIGN) * ROW_ALIGN
    lo = jnp.cumsum(ca, axis=1) - ca
    tot = ca.sum(axis=0)
    seg = ((tot + EXPERT_TILE - 1) // EXPERT_TILE) * EXPERT_TILE
    ends = jnp.cumsum(seg)
    offs = ends - seg
    start = offs[None, :] + jnp.cumsum(ca, axis=0) - ca
    i32 = lambda a: a.astype(jnp.int32)
    row = lambda a: jnp.pad(a.astype(F32), ((0, 0), (0, LANES - N_EXPERTS))).reshape(nb, 1, LANES)
    return dict(start=i32(start.reshape(-1)), lo=i32(lo.reshape(-1)), units=i32((ca // ROW_ALIGN).reshape(-1)),
                tail_start=i32(offs + tot), tail_units=i32((seg - tot) // ROW_ALIGN),
                n_used=i32((ends[-1] // EXPERT_TILE).reshape(1)),
                seg_first=i32(offs // EXPERT_TILE), seg_count=i32(seg // EXPERT_TILE),
                lo_row=row(lo), hi_row=row(lo + ca))


def _rope_tables(length):
    rows = length // GRID_W
    r = jnp.broadcast_to(jnp.arange(rows, dtype=F32)[:, None], (rows, GRID_W)).reshape(-1)
    col = jnp.broadcast_to(jnp.arange(GRID_W, dtype=F32)[None, :], (rows, GRID_W)).reshape(-1)
    n_freq = RET_QK_DIM // 4
    inv = ROPE_BASE ** (-jnp.arange(n_freq, dtype=F32) / n_freq)
    ang = jnp.concatenate([r[:, None] * inv, col[:, None] * inv], axis=-1)
    cos = jnp.repeat(jnp.cos(ang), 2, axis=-1)
    sin = jnp.repeat(jnp.sin(ang), 2, axis=-1)
    sign = jnp.where(jnp.arange(RET_QK_DIM) % 2 == 0, -1.0, 1.0).astype(F32)
    return cos, sin * sign


def _pack_pairs(s):
    lead = s.shape[:-3]
    h, n, p = s.shape[-3:]
    s = s.reshape(lead + (h // 2, 2, n, p))
    s = jnp.moveaxis(s, -3, -2)
    return s.reshape(lead + (h // 2, n, 2 * p))


def _unpack_pairs(s):
    lead = s.shape[:-3]
    hp, n, p2 = s.shape[-3:]
    s = s.reshape(lead + (hp, n, 2, p2 // 2))
    s = jnp.moveaxis(s, -2, -3)
    return s.reshape(lead + (hp * 2, n, p2 // 2))


def kernel(x_prompt, x_sample, state_ssd, state_ret, state_hgrn, c, c_ctx, mod_w, mod_b, norm_mix_w, norm_ffn_w, even_in_w, ssd_conv_w, ssd_conv_b, ssd_dt_bias, ssd_a_log, ssd_d, ssd_norm_w, ret_decay_logit, even_out_w, odd_in_w, hgrn_lower_bound, hgrn_norm_w, odd_out_w, router_w, router_b, exp_w_gate, exp_b_gate, exp_w_up, exp_b_up, exp_w_down, exp_b_down, final_norm_w):
    bp, lp, d = x_prompt.shape
    bs, ls, _ = x_sample.shape
    depth = mod_w.shape[0]
    tp, ts = bp * lp, bs * ls
    t = tp + ts
    assert lp == TOKEN_BLOCK and ls % TOKEN_BLOCK == 0 and tp % ls == 0 and d == D_MODEL
    assert 1 + bs <= COND_ROWS
    prompt_blocks = tp // TOKEN_BLOCK
    cond_map = functools.partial(_cond_row, prompt_blocks=prompt_blocks, blocks_per_request=ls // TOKEN_BLOCK)

    x = jnp.concatenate([x_prompt.reshape(tp, d), x_sample.reshape(ts, d)], axis=0)
    cond = jnp.zeros((COND_ROWS, d), F32).at[0].set(c_ctx).at[1:1 + bs].set(c)
    mod = _modulation(cond, mod_w, mod_b)
    mod = mod.reshape(depth, COND_ROWS, 1, 6 * d)

    rope = _rope_tables(ls)
    n_tiles = -(-(t * TOP_K + (t // TOKEN_BLOCK) * N_EXPERTS * (ROW_ALIGN - 1)
                  + N_EXPERTS * (EXPERT_TILE - 1)) // EXPERT_TILE)

    lb = jnp.cumsum(jax.nn.softmax(hgrn_lower_bound.astype(F32), axis=0), axis=0)
    lb = lb - lb[0]
    log_lb = jnp.log(lb)
    log1m_lb = jnp.log1p(-lb)

    new_ssd, new_ret, new_hgrn = [], [], []
    for layer in range(depth):
        j = layer // 2
        mod_l = mod[layer]
        nw_mix = norm_mix_w[layer].reshape(1, d)
        if layer % 2 == 0:
            w = even_in_w[j]
            o = 0
            cols = {}
            for name, n in (("z", SSD_INNER), ("xbc", SSD_CONV_CH), ("dt", 2 * SSD_HEADS), ("q", RET_QK),
                            ("k", RET_QK), ("v", RET_V), ("g", RET_V)):
                cols[name] = w[:, o:o + n]
                o += n
            dt_pad = jnp.pad(cols["dt"], ((0, 0), (0, LANES - 2 * SSD_HEADS)))
            w_r = jnp.concatenate([cols["z"], cols["xbc"], cols["q"], cols["k"], cols["v"], cols["g"], dt_pad],
                                  axis=1).astype(BF16)
            z, xbc, q, k, v, g, dt_raw = _inproj(
                x, mod_l, nw_mix, w_r,
                (SSD_INNER, SSD_CONV_CH, RET_QK, RET_QK, RET_V, RET_V, LANES),
                (F32, F32, F32, F32, BF16, F32, F32), cond_map)
            pad32 = lambda a: jnp.pad(a.reshape(1, -1).astype(F32), ((0, 0), (0, LANES - 2 * SSD_HEADS)))
            ssd_args = (xbc, dt_raw, ssd_conv_w[j], ssd_conv_b[j].reshape(1, -1), pad32(ssd_dt_bias[j]),
                        pad32(ssd_a_log[j]), jnp.repeat(ssd_d[j], SSD_HEAD_DIM).reshape(1, -1))
            y_ssd, st_ssd = _ssd_scan(*ssd_args, row0=0, n_seq=bp, seq_len=lp, h0=None,
                                      prev_out=jnp.zeros((t, SSD_INNER), F32))
            h0_ssd = _pack_pairs(state_ssd[:, j])
            (y_ssd,) = _ssd_scan(*ssd_args, row0=tp, n_seq=bs, seq_len=ls, h0=h0_ssd, prev_out=y_ssd)
            new_ssd.append(_unpack_pairs(st_ssd))
            lg_rows = jnp.broadcast_to(ret_decay_logit[j].reshape(2 * RET_HEADS, 1).astype(F32),
                                       (2 * RET_HEADS, LANES))
            y_ret, st_ret = _ret_scan(q, k, v, g, lg_rows, None, row0=0, n_seq=bp, seq_len=lp, h0=None,
                                      prev_out=jnp.zeros((t, RET_V), BF16), layer_j=j)
            (y_ret,) = _ret_scan(q, k, v, g, lg_rows, rope, row0=tp, n_seq=bs, seq_len=ls, h0=state_ret,
                                 prev_out=y_ret, layer_j=j)
            new_ret.append(st_ret)
            x = _even_out(y_ssd, z, ssd_norm_w[j].reshape(1, -1), y_ret, even_out_w[j].astype(BF16), x, mod_l,
                          cond_map)
        else:
            w_r = odd_in_w[j].astype(BF16)
            n = HGRN_HEADS * HGRN_HEAD_DIM
            q, ff, fb, iv, g = _inproj(x, mod_l, nw_mix, w_r, (n,) * 5, (F32, F32, F32, BF16, F32), cond_map)
            hargs = (q, ff, fb, iv, g, log_lb[j], log1m_lb[j], hgrn_norm_w[j].reshape(1, -1))
            o_h, st_h = _hgrn_scan(*hargs, row0=0, n_seq=bp, seq_len=lp, h0=None,
                                   prev_out=jnp.zeros((t, n), BF16), layer_j=j)
            (o_h,) = _hgrn_scan(*hargs, row0=tp, n_seq=bs, seq_len=ls, h0=state_hgrn, prev_out=o_h, layer_j=j)
            new_hgrn.append(st_h)
            x = _odd_out(o_h, odd_out_w[j].astype(BF16), x, mod_l, cond_map)

        rw_pad = jnp.pad(router_w[layer], ((0, 0), (0, LANES - N_EXPERTS)))
        rb_pad = jnp.pad(router_b[layer].reshape(1, -1), ((0, 0), (0, LANES - N_EXPERTS)), constant_values=NEG_BIG)
        hffn, gates, rank, cnt_tiles = _router(x, mod_l, norm_ffn_w[layer].reshape(1, d), rw_pad, rb_pad, cond_map)
        tables = _routing_tables(cnt_tiles, n_tiles)
        x_sorted = _dispatch(hffn, rank, tables, n_tiles * EXPERT_TILE)
        e, dff = exp_b_gate.shape[1], exp_b_gate.shape[2]
        y_sorted = _experts(x_sorted, tables["seg_first"], tables["seg_count"], tables["n_used"], exp_w_gate,
                            exp_w_up, exp_w_down,
                            exp_b_gate.reshape(depth, e, 1, dff), exp_b_up.reshape(depth, e, 1, dff),
                            exp_b_down.reshape(depth, e, 1, d), layer)
        x = _combine(y_sorted, rank, gates, tables, x, mod_l, final_norm_w.reshape(1, d), cond_map,
                     final_norm=(layer == depth - 1))

    y_prompt = x[:tp].reshape(bp, lp, d)
    y_sample = x[tp:].reshape(bs, ls, d)
    return (y_prompt, y_sample, jnp.stack(new_ssd, axis=1), jnp.stack(new_ret, axis=1),
            jnp.stack(new_hgrn, axis=1))
```

```python
import functools
import math

import jax
import jax.numpy as jnp
from jax import lax
from jax.experimental import pallas as pl
from jax.experimental.pallas import tpu as pltpu

F32 = jnp.float32
BF16 = jnp.bfloat16
HIGHEST = lax.Precision.HIGHEST

D_MODEL = 1024
GRID_W = 64
SSD_HEADS = 16
SSD_HEAD_DIM = 64
SSD_INNER = SSD_HEADS * SSD_HEAD_DIM
SSD_STATE = 128
SSD_GROUPS = 2
SSD_CONV_CH = SSD_INNER + 2 * SSD_GROUPS * SSD_STATE
RET_HEADS = 4
RET_QK_DIM = 128
RET_V_DIM = 256
RET_QK = RET_HEADS * RET_QK_DIM
RET_V = RET_HEADS * RET_V_DIM
ROPE_BASE = 10000.0
HGRN_HEADS = 8
HGRN_HEAD_DIM = 128
N_EXPERTS = 32
TOP_K = 4
SWIGLU_LIMIT = 7.0
SWIGLU_ALPHA = 1.702
NORM_EPS = 1e-6

LANES = 128
SUBLANES = 8
TOKEN_BLOCK = 256
SCAN_CHUNK = 256
EXPERT_TILE = 256
ROW_ALIGN = 16
PACK_ROWS = -(-(TOKEN_BLOCK * TOP_K + N_EXPERTS * (ROW_ALIGN - 1)) // 256) * 256
GROUP_CHUNK_BITS = (TOKEN_BLOCK // ROW_ALIGN).bit_length()
TAIL_CHUNK_BITS = (EXPERT_TILE // ROW_ALIGN - 1).bit_length()
COND_ROWS = 16
NEG_BIG = -1e30
LOG2E = math.log2(math.e)
VMEM_LIMIT = 56 * 1024 * 1024


def _cparams(n_axes):
    return pltpu.CompilerParams(dimension_semantics=("arbitrary",) * n_axes,
                                vmem_limit_bytes=VMEM_LIMIT)


def _sigmoid(x):
    return 1.0 / (1.0 + jnp.exp(-x))


def _silu(x):
    return x * _sigmoid(x)


def _softplus(x):
    return jnp.maximum(x, 0.0) + jnp.log(1.0 + jnp.exp(-jnp.abs(x)))


def _log_sigmoid(x):
    return jnp.minimum(x, 0.0) - jnp.log(1.0 + jnp.exp(-jnp.abs(x)))


def _dot(a, b):
    return jnp.dot(a, b, preferred_element_type=F32)


def _dot_nt(a, b):
    return lax.dot_general(a, b, (((1,), (1,)), ((), ())), preferred_element_type=F32)


def _dot_tn(a, b):
    return lax.dot_general(a, b, (((0,), (0,)), ((), ())), preferred_element_type=F32)


def _cond_row(i, prompt_blocks, blocks_per_request):
    return jnp.where(i < prompt_blocks, 0, 1 + (i - prompt_blocks) // blocks_per_request)


def _mod_kernel(cond_ref, w_ref, b_ref, o_ref):
    sc = _silu(cond_ref[...]).astype(BF16)
    o_ref[...] = _dot(sc, w_ref[...].astype(BF16)) + b_ref[...]


def _modulation(cond, mod_w, mod_b):
    depth, d, n = mod_w.shape
    tn = 1536
    return pl.pallas_call(
        _mod_kernel,
        grid=(depth, n // tn),
        in_specs=[pl.BlockSpec((COND_ROWS, d), lambda l, j: (0, 0)),
                  pl.BlockSpec((None, d, tn), lambda l, j: (l, 0, j)),
                  pl.BlockSpec((None, 1, tn), lambda l, j: (l, 0, j))],
        out_specs=pl.BlockSpec((None, COND_ROWS, tn), lambda l, j: (l, 0, j)),
        out_shape=jax.ShapeDtypeStruct((depth, COND_ROWS, n), F32),
        compiler_params=_cparams(2),
        name="modulation",
    )(cond, mod_w, mod_b.reshape(depth, 1, n))


def _norm_mod(x, nw, mod, shift_idx, scale_idx):
    d = x.shape[-1]
    xn = x * lax.rsqrt(jnp.mean(x * x, axis=-1, keepdims=True) + NORM_EPS) * nw
    sh = mod[:, shift_idx * d:(shift_idx + 1) * d]
    sc = mod[:, scale_idx * d:(scale_idx + 1) * d]
    return xn * (1.0 + sc) + sh


def _inproj_kernel(x_ref, mod_ref, nw_ref, w_ref, *out_refs, splits):
    h = _norm_mod(x_ref[...], nw_ref[...], mod_ref[...], 0, 1).astype(BF16)
    off = 0
    for o_ref, n in zip(out_refs, splits):
        o_ref[...] = _dot(h, w_ref[:, off:off + n]).astype(o_ref.dtype)
        off += n


def _inproj(x, mod_l, nw, w, splits, dtypes, cond_map):
    t, d = x.shape
    n = w.shape[1]
    assert sum(splits) == n
    nb = t // TOKEN_BLOCK
    return pl.pallas_call(
        functools.partial(_inproj_kernel, splits=tuple(splits)),
        grid=(nb,),
        in_specs=[pl.BlockSpec((TOKEN_BLOCK, d), lambda i: (i, 0)),
                  pl.BlockSpec((None, 1, mod_l.shape[-1]), lambda i: (cond_map(i), 0, 0)),
                  pl.BlockSpec((1, d), lambda i: (0, 0)),
                  pl.BlockSpec((d, n), lambda i: (0, 0))],
        out_specs=[pl.BlockSpec((TOKEN_BLOCK, s), lambda i: (i, 0)) for s in splits],
        out_shape=[jax.ShapeDtypeStruct((t, s), dt) for s, dt in zip(splits, dtypes)],
        compiler_params=_cparams(1),
        name="inproj",
    )(x, mod_l, nw, w)


def _even_out_kernel(yssd_ref, z_ref, nw_ref, yret_ref, w_ref, x_ref, mod_ref, o_ref):
    d = x_ref.shape[-1]
    u = yssd_ref[...] * _silu(z_ref[...])
    un = u * lax.rsqrt(jnp.mean(u * u, axis=-1, keepdims=True) + NORM_EPS) * nw_ref[...]
    k1 = yssd_ref.shape[-1]
    acc = _dot(un.astype(BF16), w_ref[0:k1, :]) + _dot(yret_ref[...], w_ref[k1:, :])
    o_ref[...] = x_ref[...] + mod_ref[:, 2 * d:3 * d] * acc


def _even_out(yssd, z, nw, yret, w, x, mod_l, cond_map):
    t, d = x.shape
    nb = t // TOKEN_BLOCK
    row = lambda i: (i, 0)
    const = lambda i: (0, 0)
    return pl.pallas_call(
        _even_out_kernel,
        grid=(nb,),
        in_specs=[pl.BlockSpec((TOKEN_BLOCK, yssd.shape[1]), row),
                  pl.BlockSpec((TOKEN_BLOCK, z.shape[1]), row),
                  pl.BlockSpec((1, nw.shape[1]), const),
                  pl.BlockSpec((TOKEN_BLOCK, yret.shape[1]), row),
                  pl.BlockSpec(w.shape, const),
                  pl.BlockSpec((TOKEN_BLOCK, d), row),
                  pl.BlockSpec((None, 1, mod_l.shape[-1]), lambda i: (cond_map(i), 0, 0))],
        out_specs=pl.BlockSpec((TOKEN_BLOCK, d), row),
        out_shape=jax.ShapeDtypeStruct((t, d), F32),
        compiler_params=_cparams(1),
        name="even_out",
    )(yssd, z, nw, yret, w, x, mod_l)


def _odd_out_kernel(a_ref, w_ref, x_ref, mod_ref, o_ref):
    d = x_ref.shape[-1]
    o_ref[...] = x_ref[...] + mod_ref[:, 2 * d:3 * d] * _dot(a_ref[...], w_ref[...])


def _odd_out(a, w, x, mod_l, cond_map):
    t, d = x.shape
    nb = t // TOKEN_BLOCK
    row = lambda i: (i, 0)
    return pl.pallas_call(
        _odd_out_kernel,
        grid=(nb,),
        in_specs=[pl.BlockSpec((TOKEN_BLOCK, a.shape[1]), row),
                  pl.BlockSpec(w.shape, lambda i: (0, 0)),
                  pl.BlockSpec((TOKEN_BLOCK, d), row),
                  pl.BlockSpec((None, 1, mod_l.shape[-1]), lambda i: (cond_map(i), 0, 0))],
        out_specs=pl.BlockSpec((TOKEN_BLOCK, d), row),
        out_shape=jax.ShapeDtypeStruct((t, d), F32),
        compiler_params=_cparams(1),
        name="odd_out",
    )(a, w, x, mod_l)


def _tri(q, upper):
    r = lax.broadcasted_iota(jnp.int32, (q, q), 0)
    c = lax.broadcasted_iota(jnp.int32, (q, q), 1)
    return (c >= r) if upper else (c <= r)


def _prefix_suffix_sums(x):
    q = x.shape[0]
    m = _tri(q, upper=False).astype(BF16)
    pre = sum(_dot(m, part) for part in _split3(x))
    return pre, pre[q - 1:q, :] - pre + x


def _ssd_kernel(*refs, seq_len, has_h0, emit_state):
    it = iter(refs)
    x_ref, b_ref, c_ref = next(it), next(it), next(it)
    cwx_ref, cwb_ref, cwc_ref = next(it), next(it), next(it)
    cbx_ref, cbb_ref, cbc_ref = next(it), next(it), next(it)
    dt_ref, dtb_ref, alog_ref, dskip_ref = next(it), next(it), next(it), next(it)
    h0_ref = next(it) if has_h0 else None
    next(it)
    y_ref = next(it)
    st_ref = next(it) if emit_state else None

    L = seq_len
    q = SCAN_CHUNK
    nc = L // q
    pair = pl.program_id(1)
    half = SSD_HEAD_DIM

    row = lax.broadcasted_iota(jnp.int32, (L, LANES), 0)

    def conv_silu(ref, w_ref, bias_ref):
        x = ref[...]
        w = w_ref[...]
        prev = jnp.where(row == 0, 0.0, pltpu.roll(x, 1, axis=0))
        nxt = jnp.where(row == L - 1, 0.0, pltpu.roll(x, L - 1, axis=0))
        return _silu(prev * w[0:1, :] + x * w[1:2, :] + nxt * w[2:3, :] + bias_ref[...])

    xs = conv_silu(x_ref, cwx_ref, cbx_ref)
    bm = conv_silu(b_ref, cwb_ref, cbb_ref)
    cm = conv_silu(c_ref, cwc_ref, cbc_ref)
    xs_bf, bm_bf, cm_bf = xs.astype(BF16), bm.astype(BF16), cm.astype(BF16)

    dt = _softplus(dt_ref[...] + dtb_ref[...])
    la = -dt * jnp.exp(alog_ref[...])

    lane = lax.broadcasted_iota(jnp.int32, (q, LANES), 1)
    first = lane < half
    tril = _tri(q, upper=False)
    triu = _tri(q, upper=True)

    def pick_col(arr, col):
        return jnp.sum(jnp.where(lane == col, arr, 0.0), axis=1, keepdims=True)

    def pick_row(arr_t, r):
        rr = lax.broadcasted_iota(jnp.int32, arr_t.shape, 0)
        return jnp.sum(jnp.where(rr == r, arr_t, 0.0), axis=0, keepdims=True)

    y_parts, e_f, e_b, d_sf, d_sb, dec_f, dec_b = [], [], [], [], [], [], []
    for c in range(nc):
        sl = slice(c * q, (c + 1) * q)
        la_c, dt_c = la[sl], dt[sl]
        cum, rcs = _prefix_suffix_sums(la_c)
        cum_t, rcs_t, dt_t = cum.T, rcs.T, dt_c.T
        s_g = _dot_nt(cm_bf[sl], bm_bf[sl])
        y_c = None
        cols = {}
        for hh in range(2):
            head = 2 * pair + hh
            cf, cb = pick_col(cum, head), pick_col(rcs, SSD_HEADS + head)
            rf, rb = pick_row(cum_t, head), pick_row(rcs_t, SSD_HEADS + head)
            dtf_r, dtb_r = pick_row(dt_t, head), pick_row(dt_t, SSD_HEADS + head)
            dec = (jnp.exp(jnp.where(tril, cf - rf, NEG_BIG)) * dtf_r
                   + jnp.exp(jnp.where(triu, cb - rb, NEG_BIG)) * dtb_r)
            y_h = _dot((s_g * dec).astype(BF16), xs_bf[sl])
            y_c = y_h if hh == 0 else jnp.where(first, y_c, y_h)
            cols[hh] = (cf, cb, pick_col(dt_c, head), pick_col(dt_c, SSD_HEADS + head))
        cum_e = jnp.where(first, cols[0][0], cols[1][0])
        rcs_e = jnp.where(first, cols[0][1], cols[1][1])
        dtf_e = jnp.where(first, cols[0][2], cols[1][2])
        dtb_e = jnp.where(first, cols[0][3], cols[1][3])
        y_parts.append(y_c)
        e_f.append(jnp.exp(cum_e))
        e_b.append(jnp.exp(rcs_e))
        xs_c = xs[sl]
        wf = xs_c * dtf_e * jnp.exp(cum_e[q - 1:q, :] - cum_e)
        wb = xs_c * dtb_e * jnp.exp(rcs_e[0:1, :] - rcs_e)
        d_sf.append(_dot_tn(bm_bf[sl], wf.astype(BF16)))
        d_sb.append(_dot_tn(bm_bf[sl], wb.astype(BF16)))
        dec_f.append(jnp.exp(cum_e[q - 1:q, :]))
        dec_b.append(jnp.exp(rcs_e[0:1, :]))

    zero = jnp.zeros((SSD_STATE, LANES), F32)
    sf = h0_ref[0] if has_h0 else zero
    sf_in = []
    for c in range(nc):
        sf_in.append(sf)
        sf = dec_f[c] * sf + d_sf[c]
    sb = h0_ref[1] if has_h0 else zero
    sb_in = [None] * nc
    for c in reversed(range(nc)):
        sb_in[c] = sb
        sb = dec_b[c] * sb + d_sb[c]

    dsk = dskip_ref[...]
    for c in range(nc):
        sl = slice(c * q, (c + 1) * q)
        y_c = y_parts[c] + xs[sl] * dsk
        if has_h0 or c > 0:
            y_c = y_c + e_f[c] * _dot(cm_bf[sl], sf_in[c].astype(BF16))
        if has_h0 or c < nc - 1:
            y_c = y_c + e_b[c] * _dot(cm_bf[sl], sb_in[c].astype(BF16))
        y_ref[sl, :] = y_c
    if emit_state:
        st_ref[0] = sf
        st_ref[1] = sb


def _ssd_scan(xbc, dt_raw, conv_w, conv_b, dt_bias, a_log, d_skip, *, row0, n_seq, seq_len,
              h0, prev_out):
    t = xbc.shape[0]
    blk0 = row0 // seq_len
    n_pairs = SSD_HEADS // 2
    pairs_per_group = n_pairs // SSD_GROUPS
    xoff, boff, coff = 0, SSD_INNER // LANES, (SSD_INNER + SSD_GROUPS * SSD_STATE) // LANES
    has_h0 = h0 is not None
    emit_state = not has_h0

    def seq_blk(col_fn):
        return lambda s, p: (blk0 + s, col_fn(p))

    fx = lambda p: xoff + p
    fb = lambda p: boff + p // pairs_per_group
    fc = lambda p: coff + p // pairs_per_group
    vec = lambda rows, col_fn: pl.BlockSpec((rows, LANES), lambda s, p: (0, col_fn(p)))
    in_specs = [pl.BlockSpec((seq_len, LANES), seq_blk(fx)),
                pl.BlockSpec((seq_len, LANES), seq_blk(fb)),
                pl.BlockSpec((seq_len, LANES), seq_blk(fc)),
                vec(3, fx), vec(3, fb), vec(3, fc),
                vec(1, fx), vec(1, fb), vec(1, fc),
                pl.BlockSpec((seq_len, LANES), lambda s, p: (blk0 + s, 0)),
                pl.BlockSpec((1, LANES), lambda s, p: (0, 0)),
                pl.BlockSpec((1, LANES), lambda s, p: (0, 0)),
                vec(1, fx)]
    args = [xbc, xbc, xbc, conv_w, conv_w, conv_w, conv_b, conv_b, conv_b,
            dt_raw, dt_bias, a_log, d_skip]
    io_alias = {}
    if has_h0:
        in_specs.append(pl.BlockSpec((None, 2, None, SSD_STATE, LANES), lambda s, p: (s, 0, p, 0, 0)))
        args.append(h0)
    io_alias = {len(args): 0}
    in_specs.append(pl.BlockSpec(memory_space=pl.ANY))
    args.append(prev_out)
    out_specs = [pl.BlockSpec((seq_len, LANES), lambda s, p: (blk0 + s, p))]
    out_shape = [jax.ShapeDtypeStruct((t, SSD_INNER), F32)]
    if emit_state:
        out_specs.append(pl.BlockSpec((None, 2, None, SSD_STATE, LANES), lambda s, p: (s, 0, p, 0, 0)))
        out_shape.append(jax.ShapeDtypeStruct((n_seq, 2, n_pairs, SSD_STATE, LANES), F32))
    kern = functools.partial(_ssd_kernel, seq_len=seq_len, has_h0=has_h0, emit_state=emit_state)
    return pl.pallas_call(
        kern, grid=(n_seq, n_pairs), in_specs=in_specs, out_specs=out_specs, out_shape=out_shape,
        input_output_aliases=io_alias, compiler_params=_cparams(2), name="ssd_scan",
    )(*args)


def _ret_kernel(*refs, seq_len, has_h0, emit_state, use_rope):
    it = iter(refs)
    q_ref, k_ref, v_ref, g_ref, lg_ref = next(it), next(it), next(it), next(it), next(it)
    cos_ref = next(it) if use_rope else None
    sin_ref = next(it) if use_rope else None
    h0_ref = next(it) if has_h0 else None
    next(it)
    o_ref = next(it)
    st_ref = next(it) if emit_state else None

    L = seq_len
    qn = SCAN_CHUNK
    nc = L // qn
    head = pl.program_id(1)

    qf = q_ref[...]
    kf = k_ref[...] * (RET_QK_DIM ** -0.5)
    if use_rope:
        lane = lax.broadcasted_iota(jnp.int32, (L, LANES), 1)
        even = (lane % 2) == 0
        cos, sin = cos_ref[...], sin_ref[...]

        def rope(x):
            swapped = jnp.where(even, pltpu.roll(x, LANES - 1, axis=1), pltpu.roll(x, 1, axis=1))
            return x * cos + swapped * sin
        qf, kf = rope(qf), rope(kf)
    q_bf, k_bf = qf.astype(BF16), kf.astype(BF16)
    v_bf = v_ref[...]

    lg = _log_sigmoid(lg_ref[...])
    rr = lax.broadcasted_iota(jnp.int32, lg.shape, 0)
    lgf = jnp.sum(jnp.where(rr == head, lg, 0.0), axis=0, keepdims=True)[:, 0:1]
    lgb = jnp.sum(jnp.where(rr == RET_HEADS + head, lg, 0.0), axis=0, keepdims=True)[:, 0:1]

    ti = lax.broadcasted_iota(jnp.int32, (qn, qn), 0)
    si = lax.broadcasted_iota(jnp.int32, (qn, qn), 1)
    dist = (ti - si).astype(F32)
    dec = (jnp.exp(jnp.where(ti >= si, dist * lgf, NEG_BIG))
           + jnp.exp(jnp.where(si >= ti, -dist * lgb, NEG_BIG)))
    tcol = lax.broadcasted_iota(jnp.int32, (qn, 1), 0).astype(F32)
    ef = jnp.exp((tcol + 1.0) * lgf)
    eb = jnp.exp((qn - tcol) * lgb)
    wf = jnp.exp((qn - 1.0 - tcol) * lgf)
    wb = jnp.exp(tcol * lgb)
    dec_f = jnp.exp(qn * lgf)
    dec_b = jnp.exp(qn * lgb)

    y_parts, d_sf, d_sb = [], [], []
    for c in range(nc):
        sl = slice(c * qn, (c + 1) * qn)
        s = _dot_nt(q_bf[sl], k_bf[sl])
        y_parts.append(_dot((s * dec).astype(BF16), v_bf[sl]))
        d_sf.append(_dot_tn((kf[sl] * wf).astype(BF16), v_bf[sl]))
        d_sb.append(_dot_tn((kf[sl] * wb).astype(BF16), v_bf[sl]))

    zero = jnp.zeros((RET_QK_DIM, RET_V_DIM), F32)
    sf = h0_ref[0] if has_h0 else zero
    sf_in = []
    for c in range(nc):
        sf_in.append(sf)
        sf = dec_f * sf + d_sf[c]
    sb = h0_ref[1] if has_h0 else zero
    sb_in = [None] * nc
    for c in reversed(range(nc)):
        sb_in[c] = sb
        sb = dec_b * sb + d_sb[c]

    for c in range(nc):
        sl = slice(c * qn, (c + 1) * qn)
        o = y_parts[c]
        if has_h0 or c > 0:
            o = o + ef * _dot(q_bf[sl], sf_in[c].astype(BF16))
        if has_h0 or c < nc - 1:
            o = o + eb * _dot(q_bf[sl], sb_in[c].astype(BF16))
        mu = jnp.mean(o, axis=-1, keepdims=True)
        var = jnp.mean(jnp.square(o - mu), axis=-1, keepdims=True)
        on = (o - mu) * lax.rsqrt(var + NORM_EPS)
        o_ref[sl, :] = (on * _silu(g_ref[sl, :])).astype(o_ref.dtype)
    if emit_state:
        st_ref[0] = sf
        st_ref[1] = sb


def _ret_scan(q, k, v, g, lg_rows, rope, *, row0, n_seq, seq_len, h0, prev_out, layer_j):
    t = q.shape[0]
    blk0 = row0 // seq_len
    has_h0 = h0 is not None
    emit_state = not has_h0
    use_rope = rope is not None
    blk = lambda w: pl.BlockSpec((seq_len, w), lambda s, h: (blk0 + s, h))
    in_specs = [blk(RET_QK_DIM), blk(RET_QK_DIM), blk(RET_V_DIM), blk(RET_V_DIM),
                pl.BlockSpec(lg_rows.shape, lambda s, h: (0, 0))]
    args = [q, k, v, g, lg_rows]
    if use_rope:
        in_specs += [pl.BlockSpec((seq_len, LANES), lambda s, h: (0, 0))] * 2
        args += [rope[0], rope[1]]
    if has_h0:
        in_specs.append(pl.BlockSpec((None, None, 2, None, RET_QK_DIM, RET_V_DIM),
                                     lambda s, h: (s, layer_j, 0, h, 0, 0)))
        args.append(h0)
    io_alias = {}
    io_alias = {len(args): 0}
    in_specs.append(pl.BlockSpec(memory_space=pl.ANY))
    args.append(prev_out)
    out_specs = [pl.BlockSpec((seq_len, RET_V_DIM), lambda s, h: (blk0 + s, h))]
    out_shape = [jax.ShapeDtypeStruct((t, RET_V), BF16)]
    if emit_state:
        out_specs.append(pl.BlockSpec((None, 2, None, RET_QK_DIM, RET_V_DIM), lambda s, h: (s, 0, h, 0, 0)))
        out_shape.append(jax.ShapeDtypeStruct((n_seq, 2, RET_HEADS, RET_QK_DIM, RET_V_DIM), F32))
    kern = functools.partial(_ret_kernel, seq_len=seq_len, has_h0=has_h0, emit_state=emit_state,
                             use_rope=use_rope)
    return pl.pallas_call(
        kern, grid=(n_seq, RET_HEADS), in_specs=in_specs, out_specs=out_specs, out_shape=out_shape,
        input_output_aliases=io_alias, compiler_params=_cparams(2), name="ret_scan",
    )(*args)


def _pair_boundaries(block_last, block_first, upper, h):
    qn = block_last.shape[0]
    last_prev = pltpu.roll(block_last, h, axis=0)
    last_next = pltpu.roll(block_last, qn - h, axis=0)
    first_prev = pltpu.roll(block_first, h, axis=0)
    first_next = pltpu.roll(block_first, qn - h, axis=0)
    ref_f = jnp.where(upper, last_prev, block_last)
    ref_b = jnp.where(upper, block_first, first_next)
    return ref_f, ref_b, jnp.where(upper, block_last, last_next), jnp.where(upper, first_prev, block_first)


def _hgrn_kernel(*refs, seq_len, has_h0, emit_state):
    it = iter(refs)
    q_ref, ff_ref, fb_ref, i_ref, g_ref = next(it), next(it), next(it), next(it), next(it)
    llb_ref, l1m_ref, nw_ref = next(it), next(it), next(it)
    h0_ref = next(it) if has_h0 else None
    next(it)
    o_ref = next(it)
    st_ref = next(it) if emit_state else None

    L = seq_len
    qn = SCAN_CHUNK
    nc = L // qn

    def log_forget(fz, d):
        a = llb_ref[d:d + 1, :]
        b = l1m_ref[d:d + 1, :] + _log_sigmoid(fz)
        m = jnp.maximum(a, b)
        return m + jnp.log(1.0 + jnp.exp(-jnp.abs(a - b)))

    lf_f = log_forget(ff_ref[...], 0)
    lf_b = log_forget(fb_ref[...], 1)
    k_f = 1.0 - jnp.exp(lf_f)
    k_b = 1.0 - jnp.exp(lf_b)
    qv = q_ref[...]
    v_bf = i_ref[...]

    t_col = lax.broadcasted_iota(jnp.int32, (qn, 1), 0)
    ti = lax.broadcasted_iota(jnp.int32, (qn, qn), 0)
    si = lax.broadcasted_iota(jnp.int32, (qn, qn), 1)
    pair_level = 31 - lax.clz(ti ^ si)
    n = HGRN_HEAD_DIM

    y_parts, e_f, e_b, d_sf, d_sb, dec_f, dec_b = [], [], [], [], [], [], []
    for c in range(nc):
        sl = slice(c * qn, (c + 1) * qn)
        q_c, kf_c, kb_c = qv[sl], k_f[sl], k_b[sl]
        pre, suf = _prefix_suffix_sums(jnp.concatenate([lf_f[sl], lf_b[sl]], axis=1))
        cum, rcs = pre[:, :n], suf[:, n:]
        scores = jnp.zeros((qn, qn), F32)
        cum2, rcs2 = cum * LOG2E, rcs * LOG2E
        block_last, block_first = cum2, rcs2
        h = 1
        while h < qn:
            upper = ((t_col // h) % 2) == 1
            ref_f, ref_b, block_last, block_first = _pair_boundaries(block_last, block_first, upper, h)
            d_f, d_b = cum2 - ref_f, rcs2 - ref_b
            aq = jnp.exp2(jnp.where(upper, d_f, d_b))
            ak = jnp.exp2(-jnp.where(upper, d_b, d_f))
            qt = (q_c * aq).astype(BF16)
            kt = (jnp.where(upper, kb_c, kf_c) * ak).astype(BF16)
            s_l = _dot_nt(qt, kt)
            scores = jnp.where(pair_level == h.bit_length() - 1, s_l, scores)
            h *= 2
        diag = jnp.sum(q_c * (kf_c + kb_c), axis=-1, keepdims=True)
        y_parts.append(_dot(scores.astype(BF16), v_bf[sl]) + diag * v_bf[sl].astype(F32))
        e_f.append(jnp.exp(cum))
        e_b.append(jnp.exp(rcs))
        d_sf.append(_dot_tn((kf_c * jnp.exp(cum[qn - 1:qn, :] - cum)).astype(BF16), v_bf[sl]))
        d_sb.append(_dot_tn((kb_c * jnp.exp(rcs[0:1, :] - rcs)).astype(BF16), v_bf[sl]))
        dec_f.append(jnp.broadcast_to(jnp.exp(cum[qn - 1:qn, :]), (n, n)).T)
        dec_b.append(jnp.broadcast_to(jnp.exp(rcs[0:1, :]), (n, n)).T)

    zero = jnp.zeros((HGRN_HEAD_DIM, HGRN_HEAD_DIM), F32)
    sf = h0_ref[0] if has_h0 else zero
    sf_in = []
    for c in range(nc):
        sf_in.append(sf)
        sf = dec_f[c] * sf + d_sf[c]
    sb = h0_ref[1] if has_h0 else zero
    sb_in = [None] * nc
    for c in reversed(range(nc)):
        sb_in[c] = sb
        sb = dec_b[c] * sb + d_sb[c]

    nw = nw_ref[...]
    for c in range(nc):
        sl = slice(c * qn, (c + 1) * qn)
        o = y_parts[c]
        if has_h0 or c > 0:
            o = o + _dot((qv[sl] * e_f[c]).astype(BF16), sf_in[c].astype(BF16))
        if has_h0 or c < nc - 1:
            o = o + _dot((qv[sl] * e_b[c]).astype(BF16), sb_in[c].astype(BF16))
        on = o * lax.rsqrt(jnp.mean(o * o, axis=-1, keepdims=True) + NORM_EPS) * nw
        o_ref[sl, :] = (on * _silu(g_ref[sl, :])).astype(o_ref.dtype)
    if emit_state:
        st_ref[0] = sf
        st_ref[1] = sb


def _hgrn_scan(q, ff, fb, iv, g, log_lb, log1m_lb, nw, *, row0, n_seq, seq_len, h0, prev_out, layer_j):
    t = q.shape[0]
    blk0 = row0 // seq_len
    has_h0 = h0 is not None
    emit_state = not has_h0
    n = HGRN_HEAD_DIM
    blk = pl.BlockSpec((seq_len, n), lambda s, h: (blk0 + s, h))
    in_specs = [blk, blk, blk, blk, blk,
                pl.BlockSpec((2, n), lambda s, h: (0, h)),
                pl.BlockSpec((2, n), lambda s, h: (0, h)),
                pl.BlockSpec((1, n), lambda s, h: (0, h))]
    args = [q, ff, fb, iv, g, log_lb, log1m_lb, nw]
    if has_h0:
        in_specs.append(pl.BlockSpec((None, None, 2, None, n, n), lambda s, h: (s, layer_j, 0, h, 0, 0)))
        args.append(h0)
    io_alias = {}
    io_alias = {len(args): 0}
    in_specs.append(pl.BlockSpec(memory_space=pl.ANY))
    args.append(prev_out)
    out_specs = [pl.BlockSpec((seq_len, n), lambda s, h: (blk0 + s, h))]
    out_shape = [jax.ShapeDtypeStruct((t, HGRN_HEADS * n), BF16)]
    if emit_state:
        out_specs.append(pl.BlockSpec((None, 2, None, n, n), lambda s, h: (s, 0, h, 0, 0)))
        out_shape.append(jax.ShapeDtypeStruct((n_seq, 2, HGRN_HEADS, n, n), F32))
    kern = functools.partial(_hgrn_kernel, seq_len=seq_len, has_h0=has_h0, emit_state=emit_state)
    return pl.pallas_call(
        kern, grid=(n_seq, HGRN_HEADS), in_specs=in_specs, out_specs=out_specs, out_shape=out_shape,
        input_output_aliases=io_alias, compiler_params=_cparams(2), name="hgrn_scan",
    )(*args)


def _router_kernel(x_ref, mod_ref, nw_ref, rw_ref, rb_ref, h_ref, gate_ref, rank_ref, cnt_ref):
    h = _norm_mod(x_ref[...], nw_ref[...], mod_ref[...], 3, 4)
    h_ref[...] = h.astype(BF16)
    logits = jnp.dot(h, rw_ref[...], precision=HIGHEST, preferred_element_type=F32) + rb_ref[...]
    lane = lax.broadcasted_iota(jnp.int32, logits.shape, 1).astype(F32)
    work = logits
    vals, idxs = [], []
    for _ in range(TOP_K):
        m = jnp.max(work, axis=-1, keepdims=True)
        idx = jnp.min(jnp.where(work == m, lane, float(LANES)), axis=-1, keepdims=True)
        vals.append(m)
        idxs.append(idx)
        work = jnp.where(lane == idx, -jnp.inf, work)
    es = [jnp.exp(v - vals[0]) for v in vals]
    denom = es[0] + es[1] + es[2] + es[3]
    gates = jnp.zeros(logits.shape, F32)
    member = jnp.zeros(logits.shape, F32)
    for k in range(TOP_K):
        hit = lane == idxs[k]
        gates = jnp.where(hit, es[k] / denom, gates)
        member = jnp.where(hit, 1.0, member)
    tb = logits.shape[0]
    before = _tri(tb, upper=False) & ~_tri(tb, upper=True)
    rank = _dot(before.astype(BF16), member.astype(BF16))
    gate_ref[...] = gates
    rank_ref[...] = jnp.where(member > 0.0, rank, -1.0)
    cnt_ref[...] = jnp.sum(member, axis=0, keepdims=True).astype(jnp.int32)


def _router(x, mod_l, nw, rw_pad, rb_pad, cond_map):
    t, d = x.shape
    nb = t // TOKEN_BLOCK
    row = lambda i: (i, 0)
    const = lambda i: (0, 0)
    return pl.pallas_call(
        _router_kernel,
        grid=(nb,),
        in_specs=[pl.BlockSpec((TOKEN_BLOCK, d), row),
                  pl.BlockSpec((None, 1, mod_l.shape[-1]), lambda i: (cond_map(i), 0, 0)),
                  pl.BlockSpec((1, d), const),
                  pl.BlockSpec((d, LANES), const),
                  pl.BlockSpec((1, LANES), const)],
        out_specs=[pl.BlockSpec((TOKEN_BLOCK, d), row),
                   pl.BlockSpec((TOKEN_BLOCK, LANES), row),
                   pl.BlockSpec((TOKEN_BLOCK, LANES), row),
                   pl.BlockSpec((None, 1, LANES), lambda i: (i, 0, 0))],
        out_shape=[jax.ShapeDtypeStruct((t, d), BF16),
                   jax.ShapeDtypeStruct((t, LANES), F32),
                   jax.ShapeDtypeStruct((t, LANES), F32),
                   jax.ShapeDtypeStruct((nb, 1, LANES), jnp.int32)],
        compiler_params=_cparams(1),
        name="router",
    )(x, mod_l, nw, rw_pad, rb_pad)


def _split2(x):
    hi = x.astype(BF16)
    return hi, (x - hi.astype(F32)).astype(BF16)


def _split3(x):
    hi, mid = _split2(x)
    r2 = x - hi.astype(F32) - mid.astype(F32)
    return hi, mid, r2.astype(BF16)


def _packed_owner(lo_row, hi_row):
    r = lax.broadcasted_iota(jnp.int32, (PACK_ROWS, LANES), 0).astype(F32)
    return (r >= lo_row) & (r < hi_row)


def _chunks(units, bits):
    for b in range(bits):
        yield (((units >> b) & 1) == 1,
               pl.multiple_of(((units >> (b + 1)) << (b + 1)) * ROW_ALIGN, ROW_ALIGN),
               ROW_ALIGN << b)


def _group_chunks(i, e, start_ref, lo_ref, units_ref):
    idx = i * N_EXPERTS + e
    for present, base, rows in _chunks(units_ref[idx], GROUP_CHUNK_BITS):
        yield (present, pl.multiple_of(lo_ref[idx] + base, ROW_ALIGN),
               pl.multiple_of(start_ref[idx] + base, ROW_ALIGN), rows)


def _zero_unused_rows(tail_start_ref, tail_units_ref, nu_ref, xs_hbm, zeros_buf, sem):
    n_tiles = xs_hbm.shape[0] // EXPERT_TILE
    zeros_buf[...] = jnp.zeros_like(zeros_buf)

    def tail_copy(off, rows):
        return pltpu.make_async_copy(zeros_buf.at[pl.ds(0, rows)], xs_hbm.at[pl.ds(off, rows)], sem)

    def tile_copy(n):
        return pltpu.make_async_copy(zeros_buf, xs_hbm.at[pl.ds(pl.multiple_of(n * EXPERT_TILE, EXPERT_TILE),
                                                                EXPERT_TILE)], sem)

    def each(fn):
        for e in range(N_EXPERTS):
            for present, base, rows in _chunks(tail_units_ref[e], TAIL_CHUNK_BITS):
                @pl.when(present)
                def _():
                    fn(tail_copy(pl.multiple_of(tail_start_ref[e] + base, ROW_ALIGN), rows))

        def body(n, carry):
            fn(tile_copy(n))
            return carry
        lax.fori_loop(nu_ref[0], n_tiles, body, 0)

    each(lambda c: c.start())
    each(lambda c: c.wait())


def _dispatch_kernel(start_ref, lo_ref, units_ref, tail_start_ref, tail_units_ref, nu_ref,
                     h_ref, rank_ref, lorow_ref, hirow_ref, xs_hbm, xs_buf, zeros_buf, sem):
    i = pl.program_id(0)

    @pl.when(i == 0)
    def _():
        _zero_unused_rows(tail_start_ref, tail_units_ref, nu_ref, xs_hbm, zeros_buf, sem)

    lo_row = lorow_ref[...]
    owner = _packed_owner(lo_row, hirow_ref[...])
    owned = jnp.sum(owner.astype(F32), axis=1, keepdims=True) > 0.0
    r = lax.broadcasted_iota(jnp.int32, (PACK_ROWS, 1), 0).astype(F32)
    local = r - jnp.sum(jnp.where(owner, lo_row, 0.0), axis=1, keepdims=True)
    rank_rows = _dot_nt(owner.astype(BF16), rank_ref[...].astype(BF16))
    sel = ((rank_rows == local) & owned).astype(BF16)
    xs_buf[...] = _dot(sel, h_ref[...]).astype(BF16)

    def copy(v_off, s_off, rows):
        return pltpu.make_async_copy(xs_buf.at[pl.ds(v_off, rows)], xs_hbm.at[pl.ds(s_off, rows)], sem)

    for e in range(N_EXPERTS):
        for present, v_off, s_off, rows in _group_chunks(i, e, start_ref, lo_ref, units_ref):
            @pl.when(present)
            def _():
                copy(v_off, s_off, rows).start()
    for e in range(N_EXPERTS):
        for present, v_off, s_off, rows in _group_chunks(i, e, start_ref, lo_ref, units_ref):
            @pl.when(present)
            def _():
                copy(v_off, s_off, rows).wait()


def _dispatch(hffn, rank, tables, n_rows):
    t, d = hffn.shape
    nb = t // TOKEN_BLOCK
    row = lambda i, *_: (i, 0)
    vec = pl.BlockSpec((None, 1, LANES), lambda i, *_: (i, 0, 0))
    grid_spec = pltpu.PrefetchScalarGridSpec(
        num_scalar_prefetch=6,
        grid=(nb,),
        in_specs=[pl.BlockSpec((TOKEN_BLOCK, d), row), pl.BlockSpec((TOKEN_BLOCK, LANES), row), vec, vec],
        out_specs=pl.BlockSpec(memory_space=pl.ANY),
        scratch_shapes=[pltpu.VMEM((PACK_ROWS, d), BF16), pltpu.VMEM((EXPERT_TILE, d), BF16),
                        pltpu.SemaphoreType.DMA(())],
    )
    return pl.pallas_call(
        _dispatch_kernel, grid_spec=grid_spec,
        out_shape=jax.ShapeDtypeStruct((n_rows, d), BF16),
        compiler_params=_cparams(1), name="dispatch",
    )(tables["start"], tables["lo"], tables["units"], tables["tail_start"], tables["tail_units"],
      tables["n_used"], hffn, rank, tables["lo_row"], tables["hi_row"])


def _expert_kernel(first_ref, count_ref, nu_ref, x_hbm, wg_ref, wu_ref, wd_ref, bg_ref, bu_ref, bd_ref,
                   y_hbm, xbuf, ybuf, wg_bf, wu_bf, wd_bf, sem_in, sem_out):
    e = pl.program_id(0)
    first, count = first_ref[e], count_ref[e]
    n_tiles = y_hbm.shape[0] // EXPERT_TILE

    def rows(tile):
        return pl.ds(pl.multiple_of(tile * EXPERT_TILE, EXPERT_TILE), EXPERT_TILE)

    def x_copy(j, slot):
        return pltpu.make_async_copy(x_hbm.at[rows(first + j)], xbuf.at[slot], sem_in.at[slot])

    def y_copy(tile, slot):
        return pltpu.make_async_copy(ybuf.at[slot], y_hbm.at[rows(tile)], sem_out.at[slot])

    @pl.when(count > 0)
    def _():
        x_copy(0, 0).start(priority=1)

    wg_bf[...] = wg_ref[...].astype(BF16)
    wu_bf[...] = wu_ref[...].astype(BF16)
    wd_bf[...] = wd_ref[...].astype(BF16)

    def one_tile(j, carry):
        slot = j % 2

        @pl.when(j + 1 < count)
        def _():
            x_copy(j + 1, 1 - slot).start(priority=1)

        x_copy(j, slot).wait()

        @pl.when(j >= 2)
        def _():
            y_copy(first + j - 2, slot).wait()

        x = xbuf[slot]
        gt = jnp.minimum(_dot(x, wg_bf[...]) + bg_ref[...], SWIGLU_LIMIT)
        up = jnp.clip(_dot(x, wu_bf[...]) + bu_ref[...], -SWIGLU_LIMIT, SWIGLU_LIMIT)
        act = gt * _sigmoid(SWIGLU_ALPHA * gt) * (up + 1.0)
        ybuf[slot] = (_dot(act.astype(BF16), wd_bf[...]) + bd_ref[...]).astype(ybuf.dtype)
        y_copy(first + j, slot).start(priority=1)
        return carry

    lax.fori_loop(0, count, one_tile, 0)
    for back in (2, 1):
        @pl.when(count >= back)
        def _():
            y_copy(first + count - back, (count - back) % 2).wait()

    @pl.when(e == pl.num_programs(0) - 1)
    def _():
        ybuf[0] = jnp.zeros(ybuf.shape[1:], ybuf.dtype)

        def start(tile, carry):
            y_copy(tile, 0).start()
            return carry

        def wait(tile, carry):
            y_copy(tile, 0).wait()
            return carry
        lax.fori_loop(nu_ref[0], n_tiles, start, 0)
        lax.fori_loop(nu_ref[0], n_tiles, wait, 0)


def _experts(x_sorted, seg_first, seg_count, n_used, wg, wu, wd, bg, bu, bd, layer):
    n_rows, d = x_sorted.shape
    n_experts, dff = wg.shape[1], wg.shape[-1]
    wspec = lambda a: pl.BlockSpec((None, None) + a.shape[2:], lambda e, *_: (layer, e, 0, 0))
    anyspec = pl.BlockSpec(memory_space=pl.ANY)
    grid_spec = pltpu.PrefetchScalarGridSpec(
        num_scalar_prefetch=3,
        grid=(n_experts,),
        in_specs=[anyspec, wspec(wg), wspec(wu), wspec(wd), wspec(bg), wspec(bu), wspec(bd)],
        out_specs=anyspec,
        scratch_shapes=[pltpu.VMEM((2, EXPERT_TILE, d), BF16), pltpu.VMEM((2, EXPERT_TILE, d), BF16),
                        pltpu.VMEM((d, dff), BF16), pltpu.VMEM((d, dff), BF16), pltpu.VMEM((dff, d), BF16),
                        pltpu.SemaphoreType.DMA((2,)), pltpu.SemaphoreType.DMA((2,))],
    )
    return pl.pallas_call(
        _expert_kernel, grid_spec=grid_spec,
        out_shape=jax.ShapeDtypeStruct((n_rows, d), BF16),
        compiler_params=_cparams(1), name="experts",
    )(seg_first, seg_count, n_used, x_sorted, wg, wu, wd, bg, bu, bd)


def _combine_kernel(start_ref, lo_ref, units_ref, y_hbm, rank_ref, gate_ref, lorow_ref, hirow_ref,
                    x_ref, mod_ref, fnw_ref, o_ref, buf, sem, *, final_norm):
    i = pl.program_id(0)
    d = x_ref.shape[-1]

    @pl.when(i == 0)
    def _():
        buf[...] = jnp.zeros_like(buf)

    def copy(v_off, s_off, rows):
        return pltpu.make_async_copy(y_hbm.at[pl.ds(s_off, rows)], buf.at[pl.ds(v_off, rows)], sem)

    for e in range(N_EXPERTS):
        for present, v_off, s_off, rows in _group_chunks(i, e, start_ref, lo_ref, units_ref):
            @pl.when(present)
            def _():
                copy(v_off, s_off, rows).start()

    lo_row = lorow_ref[...]
    owner = _packed_owner(lo_row, hirow_ref[...]).astype(BF16)
    rank_cols = _dot_nt(rank_ref[...].astype(BF16), owner)
    lo8 = jnp.broadcast_to(lo_row, (SUBLANES, LANES))
    lo_cols = sum(_dot_nt(part, owner) for part in _split3(lo8))[0:1, :]
    owned = _dot_nt(jnp.ones((SUBLANES, LANES), BF16), owner)[0:1, :] > 0.0
    col = lax.broadcasted_iota(jnp.int32, (1, PACK_ROWS), 1).astype(F32)
    match = (rank_cols == col - lo_cols) & owned
    g_hi, g_lo = _split2(gate_ref[...])
    q_hi = jnp.where(match, _dot_nt(g_hi, owner), 0.0).astype(BF16)
    q_lo = jnp.where(match, _dot_nt(g_lo, owner), 0.0).astype(BF16)

    for e in range(N_EXPERTS):
        for present, v_off, s_off, rows in _group_chunks(i, e, start_ref, lo_ref, units_ref):
            @pl.when(present)
            def _():
                copy(v_off, s_off, rows).wait()
    y = buf[...]
    x = x_ref[...] + mod_ref[:, 5 * d:6 * d] * (_dot(q_hi, y) + _dot(q_lo, y))
    if final_norm:
        x = x * lax.rsqrt(jnp.mean(x * x, axis=-1, keepdims=True) + NORM_EPS) * fnw_ref[...]
    o_ref[...] = x


def _combine(y_sorted, rank, gates, tables, x, mod_l, fnw, cond_map, final_norm):
    t, d = x.shape
    nb = t // TOKEN_BLOCK
    row = lambda i, *_: (i, 0)
    vec = pl.BlockSpec((None, 1, LANES), lambda i, *_: (i, 0, 0))
    grid_spec = pltpu.PrefetchScalarGridSpec(
        num_scalar_prefetch=3,
        grid=(nb,),
        in_specs=[pl.BlockSpec(memory_space=pl.ANY),
                  pl.BlockSpec((TOKEN_BLOCK, LANES), row),
                  pl.BlockSpec((TOKEN_BLOCK, LANES), row),
                  vec, vec,
                  pl.BlockSpec((TOKEN_BLOCK, d), row),
                  pl.BlockSpec((None, 1, mod_l.shape[-1]), lambda i, *_: (cond_map(i), 0, 0)),
                  pl.BlockSpec((1, d), lambda i, *_: (0, 0))],
        out_specs=pl.BlockSpec((TOKEN_BLOCK, d), row),
        scratch_shapes=[pltpu.VMEM((PACK_ROWS, d), BF16), pltpu.SemaphoreType.DMA(())],
    )
    return pl.pallas_call(
        functools.partial(_combine_kernel, final_norm=final_norm), grid_spec=grid_spec,
        out_shape=jax.ShapeDtypeStruct((t, d), F32),
        compiler_params=_cparams(1), name="combine",
    )(tables["start"], tables["lo"], tables["units"], y_sorted, rank, gates, tables["lo_row"],
      tables["hi_row"], x, mod_l, fnw)


def _routing_tables(cnt_tiles, n_tiles):
    nb = cnt_tiles.shape[0]
    c = cnt_tiles[:, 0, :N_EXPERTS]
    ca = ((c + ROW_ALIGN - 1) // ROW_ALIGN) * ROW_ALIGN
    lo = jnp.cumsum(ca, axis=1) - ca
    tot = ca.sum(axis=0)
    seg = ((tot + EXPERT_TILE - 1) // EXPERT_TILE) * EXPERT_TILE
    ends = jnp.cumsum(seg)
    offs = ends - seg
    start = offs[None, :] + jnp.cumsum(ca, axis=0) - ca
    i32 = lambda a: a.astype(jnp.int32)
    row = lambda a: jnp.pad(a.astype(F32), ((0, 0), (0, LANES - N_EXPERTS))).reshape(nb, 1, LANES)
    return dict(start=i32(start.reshape(-1)), lo=i32(lo.reshape(-1)), units=i32((ca // ROW_ALIGN).reshape(-1)),
                tail_start=i32(offs + tot), tail_units=i32((seg - tot) // ROW_ALIGN),
                n_used=i32((ends[-1] // EXPERT_TILE).reshape(1)),
                seg_first=i32(offs // EXPERT_TILE), seg_count=i32(seg // EXPERT_TILE),
                lo_row=row(lo), hi_row=row(lo + ca))


def _rope_tables(length):
    rows = length // GRID_W
    r = jnp.broadcast_to(jnp.arange(rows, dtype=F32)[:, None], (rows, GRID_W)).reshape(-1)
    col = jnp.broadcast_to(jnp.arange(GRID_W, dtype=F32)[None, :], (rows, GRID_W)).reshape(-1)
    n_freq = RET_QK_DIM // 4
    inv = ROPE_BASE ** (-jnp.arange(n_freq, dtype=F32) / n_freq)
    ang = jnp.concatenate([r[:, None] * inv, col[:, None] * inv], axis=-1)
    cos = jnp.repeat(jnp.cos(ang), 2, axis=-1)
    sin = jnp.repeat(jnp.sin(ang), 2, axis=-1)
    sign = jnp.where(jnp.arange(RET_QK_DIM) % 2 == 0, -1.0, 1.0).astype(F32)
    return cos, sin * sign


def _pack_pairs(s):
    lead = s.shape[:-3]
    h, n, p = s.shape[-3:]
    s = s.reshape(lead + (h // 2, 2, n, p))
    s = jnp.moveaxis(s, -3, -2)
    return s.reshape(lead + (h // 2, n, 2 * p))


def _unpack_pairs(s):
    lead = s.shape[:-3]
    hp, n, p2 = s.shape[-3:]
    s = s.reshape(lead + (hp, n, 2, p2 // 2))
    s = jnp.moveaxis(s, -2, -3)
    return s.reshape(lead + (hp * 2, n, p2 // 2))


def kernel(x_prompt, x_sample, state_ssd, state_ret, state_hgrn, c, c_ctx, mod_w, mod_b, norm_mix_w, norm_ffn_w, even_in_w, ssd_conv_w, ssd_conv_b, ssd_dt_bias, ssd_a_log, ssd_d, ssd_norm_w, ret_decay_logit, even_out_w, odd_in_w, hgrn_lower_bound, hgrn_norm_w, odd_out_w, router_w, router_b, exp_w_gate, exp_b_gate, exp_w_up, exp_b_up, exp_w_down, exp_b_down, final_norm_w):
    bp, lp, d = x_prompt.shape
    bs, ls, _ = x_sample.shape
    depth = mod_w.shape[0]
    tp, ts = bp * lp, bs * ls
    t = tp + ts
    assert lp == TOKEN_BLOCK and ls % TOKEN_BLOCK == 0 and tp % ls == 0 and d == D_MODEL
    assert 1 + bs <= COND_ROWS
    prompt_blocks = tp // TOKEN_BLOCK
    cond_map = functools.partial(_cond_row, prompt_blocks=prompt_blocks, blocks_per_request=ls // TOKEN_BLOCK)

    x = jnp.concatenate([x_prompt.reshape(tp, d), x_sample.reshape(ts, d)], axis=0)
    cond = jnp.zeros((COND_ROWS, d), F32).at[0].set(c_ctx).at[1:1 + bs].set(c)
    mod = _modulation(cond, mod_w, mod_b)
    mod = mod.reshape(depth, COND_ROWS, 1, 6 * d)

    rope = _rope_tables(ls)
    n_tiles = -(-(t * TOP_K + (t // TOKEN_BLOCK) * N_EXPERTS * (ROW_ALIGN - 1)
                  + N_EXPERTS * (EXPERT_TILE - 1)) // EXPERT_TILE)

    lb = jnp.cumsum(jax.nn.softmax(hgrn_lower_bound.astype(F32), axis=0), axis=0)
    lb = lb - lb[0]
    log_lb = jnp.log(lb)
    log1m_lb = jnp.log1p(-lb)

    new_ssd, new_ret, new_hgrn = [], [], []
    for layer in range(depth):
        j = layer // 2
        mod_l = mod[layer]
        nw_mix = norm_mix_w[layer].reshape(1, d)
        if layer % 2 == 0:
            w = even_in_w[j]
            o = 0
            cols = {}
            for name, n in (("z", SSD_INNER), ("xbc", SSD_CONV_CH), ("dt", 2 * SSD_HEADS), ("q", RET_QK),
                            ("k", RET_QK), ("v", RET_V), ("g", RET_V)):
                cols[name] = w[:, o:o + n]
                o += n
            dt_pad = jnp.pad(cols["dt"], ((0, 0), (0, LANES - 2 * SSD_HEADS)))
            w_r = jnp.concatenate([cols["z"], cols["xbc"], cols["q"], cols["k"], cols["v"], cols["g"], dt_pad],
                                  axis=1).astype(BF16)
            z, xbc, q, k, v, g, dt_raw = _inproj(
                x, mod_l, nw_mix, w_r,
                (SSD_INNER, SSD_CONV_CH, RET_QK, RET_QK, RET_V, RET_V, LANES),
                (F32, F32, F32, F32, BF16, F32, F32), cond_map)
            pad32 = lambda a: jnp.pad(a.reshape(1, -1).astype(F32), ((0, 0), (0, LANES - 2 * SSD_HEADS)))
            ssd_args = (xbc, dt_raw, ssd_conv_w[j], ssd_conv_b[j].reshape(1, -1), pad32(ssd_dt_bias[j]),
                        pad32(ssd_a_log[j]), jnp.repeat(ssd_d[j], SSD_HEAD_DIM).reshape(1, -1))
            y_ssd, st_ssd = _ssd_scan(*ssd_args, row0=0, n_seq=bp, seq_len=lp, h0=None,
                                      prev_out=jnp.zeros((t, SSD_INNER), F32))
            h0_ssd = _pack_pairs(state_ssd[:, j])
            (y_ssd,) = _ssd_scan(*ssd_args, row0=tp, n_seq=bs, seq_len=ls, h0=h0_ssd, prev_out=y_ssd)
            new_ssd.append(_unpack_pairs(st_ssd))
            lg_rows = jnp.broadcast_to(ret_decay_logit[j].reshape(2 * RET_HEADS, 1).astype(F32),
                                       (2 * RET_HEADS, LANES))
            y_ret, st_ret = _ret_scan(q, k, v, g, lg_rows, None, row0=0, n_seq=bp, seq_len=lp, h0=None,
                                      prev_out=jnp.zeros((t, RET_V), BF16), layer_j=j)
            (y_ret,) = _ret_scan(q, k, v, g, lg_rows, rope, row0=tp, n_seq=bs, seq_len=ls, h0=state_ret,
                                 prev_out=y_ret, layer_j=j)
            new_ret.append(st_ret)
            x = _even_out(y_ssd, z, ssd_norm_w[j].reshape(1, -1), y_ret, even_out_w[j].astype(BF16), x, mod_l,
                          cond_map)
        else:
            w_r = odd_in_w[j].astype(BF16)
            n = HGRN_HEADS * HGRN_HEAD_DIM
            q, ff, fb, iv, g = _inproj(x, mod_l, nw_mix, w_r, (n,) * 5, (F32, F32, F32, BF16, F32), cond_map)
            hargs = (q, ff, fb, iv, g, log_lb[j], log1m_lb[j], hgrn_norm_w[j].reshape(1, -1))
            o_h, st_h = _hgrn_scan(*hargs, row0=0, n_seq=bp, seq_len=lp, h0=None,
                                   prev_out=jnp.zeros((t, n), BF16), layer_j=j)
            (o_h,) = _hgrn_scan(*hargs, row0=tp, n_seq=bs, seq_len=ls, h0=state_hgrn, prev_out=o_h, layer_j=j)
            new_hgrn.append(st_h)
            x = _odd_out(o_h, odd_out_w[j].astype(BF16), x, mod_l, cond_map)

        rw_pad = jnp.pad(router_w[layer], ((0, 0), (0, LANES - N_EXPERTS)))
        rb_pad = jnp.pad(router_b[layer].reshape(1, -1), ((0, 0), (0, LANES - N_EXPERTS)), constant_values=NEG_BIG)
        hffn, gates, rank, cnt_tiles = _router(x, mod_l, norm_ffn_w[layer].reshape(1, d), rw_pad, rb_pad, cond_map)
        tables = _routing_tables(cnt_tiles, n_tiles)
        x_sorted = _dispatch(hffn, rank, tables, n_tiles * EXPERT_TILE)
        e, dff = exp_b_gate.shape[1], exp_b_gate.shape[2]
        y_sorted = _experts(x_sorted, tables["seg_first"], tables["seg_count"], tables["n_used"], exp_w_gate,
                            exp_w_up, exp_w_down,
                            exp_b_gate.reshape(depth, e, 1, dff), exp_b_up.reshape(depth, e, 1, dff),
                            exp_b_down.reshape(depth, e, 1, d), layer)
        x = _combine(y_sorted, rank, gates, tables, x, mod_l, final_norm_w.reshape(1, d), cond_map,
                     final_norm=(layer == depth - 1))

    y_prompt = x[:tp].reshape(bp, lp, d)
    y_sample = x[tp:].reshape(bs, ls, d)
    return (y_prompt, y_sample, jnp.stack(new_ssd, axis=1), jnp.stack(new_ret, axis=1),
            jnp.stack(new_hgrn, axis=1))
```

```python
import functools
import math

import jax
import jax.numpy as jnp
from jax import lax
from jax.experimental import pallas as pl
from jax.experimental.pallas import tpu as pltpu

F32 = jnp.float32
BF16 = jnp.bfloat16
HIGHEST = lax.Precision.HIGHEST

D_MODEL = 1024
GRID_W = 64
SSD_HEADS = 16
SSD_HEAD_DIM = 64
SSD_INNER = SSD_HEADS * SSD_HEAD_DIM
SSD_STATE = 128
SSD_GROUPS = 2
SSD_CONV_CH = SSD_INNER + 2 * SSD_GROUPS * SSD_STATE
RET_HEADS = 4
RET_QK_DIM = 128
RET_V_DIM = 256
RET_QK = RET_HEADS * RET_QK_DIM
RET_V = RET_HEADS * RET_V_DIM
ROPE_BASE = 10000.0
HGRN_HEADS = 8
HGRN_HEAD_DIM = 128
N_EXPERTS = 32
TOP_K = 4
SWIGLU_LIMIT = 7.0
SWIGLU_ALPHA = 1.702
NORM_EPS = 1e-6

LANES = 128
SUBLANES = 8
TOKEN_BLOCK = 256
SCAN_CHUNK = 256
EXPERT_TILE = 256
X_AHEAD = 3
ROW_ALIGN = 16
PACK_ROWS = -(-(TOKEN_BLOCK * TOP_K + N_EXPERTS * (ROW_ALIGN - 1)) // 256) * 256
GROUP_CHUNK_BITS = (TOKEN_BLOCK // ROW_ALIGN).bit_length()
TAIL_CHUNK_BITS = (EXPERT_TILE // ROW_ALIGN - 1).bit_length()
COND_ROWS = 16
NEG_BIG = -1e30
LOG2E = math.log2(math.e)
VMEM_LIMIT = 56 * 1024 * 1024


def _cparams(n_axes):
    return pltpu.CompilerParams(dimension_semantics=("arbitrary",) * n_axes,
                                vmem_limit_bytes=VMEM_LIMIT)


def _sigmoid(x):
    return 1.0 / (1.0 + jnp.exp(-x))


def _silu(x):
    return x * _sigmoid(x)


def _softplus(x):
    return jnp.maximum(x, 0.0) + jnp.log(1.0 + jnp.exp(-jnp.abs(x)))


def _log_sigmoid(x):
    return jnp.minimum(x, 0.0) - jnp.log(1.0 + jnp.exp(-jnp.abs(x)))


def _dot(a, b):
    return jnp.dot(a, b, preferred_element_type=F32)


def _dot_nt(a, b):
    return lax.dot_general(a, b, (((1,), (1,)), ((), ())), preferred_element_type=F32)


def _dot_tn(a, b):
    return lax.dot_general(a, b, (((0,), (0,)), ((), ())), preferred_element_type=F32)


def _cond_row(i, prompt_blocks, blocks_per_request):
    return jnp.where(i < prompt_blocks, 0, 1 + (i - prompt_blocks) // blocks_per_request)


def _mod_kernel(cond_ref, w_ref, b_ref, o_ref):
    sc = _silu(cond_ref[...]).astype(BF16)
    o_ref[...] = _dot(sc, w_ref[...].astype(BF16)) + b_ref[...]


def _modulation(cond, mod_w, mod_b):
    depth, d, n = mod_w.shape
    tn = 1536
    return pl.pallas_call(
        _mod_kernel,
        grid=(depth, n // tn),
        in_specs=[pl.BlockSpec((COND_ROWS, d), lambda l, j: (0, 0)),
                  pl.BlockSpec((None, d, tn), lambda l, j: (l, 0, j)),
                  pl.BlockSpec((None, 1, tn), lambda l, j: (l, 0, j))],
        out_specs=pl.BlockSpec((None, COND_ROWS, tn), lambda l, j: (l, 0, j)),
        out_shape=jax.ShapeDtypeStruct((depth, COND_ROWS, n), F32),
        compiler_params=_cparams(2),
        name="modulation",
    )(cond, mod_w, mod_b.reshape(depth, 1, n))


def _norm_mod(x, nw, mod, shift_idx, scale_idx):
    d = x.shape[-1]
    xn = x * lax.rsqrt(jnp.mean(x * x, axis=-1, keepdims=True) + NORM_EPS) * nw
    sh = mod[:, shift_idx * d:(shift_idx + 1) * d]
    sc = mod[:, scale_idx * d:(scale_idx + 1) * d]
    return xn * (1.0 + sc) + sh


def _inproj_kernel(x_ref, mod_ref, nw_ref, w_ref, *out_refs, splits):
    h = _norm_mod(x_ref[...], nw_ref[...], mod_ref[...], 0, 1).astype(BF16)
    off = 0
    for o_ref, n in zip(out_refs, splits):
        o_ref[...] = _dot(h, w_ref[:, off:off + n]).astype(o_ref.dtype)
        off += n


def _inproj(x, mod_l, nw, w, splits, dtypes, cond_map):
    t, d = x.shape
    n = w.shape[1]
    assert sum(splits) == n
    nb = t // TOKEN_BLOCK
    return pl.pallas_call(
        functools.partial(_inproj_kernel, splits=tuple(splits)),
        grid=(nb,),
        in_specs=[pl.BlockSpec((TOKEN_BLOCK, d), lambda i: (i, 0)),
                  pl.BlockSpec((None, 1, mod_l.shape[-1]), lambda i: (cond_map(i), 0, 0)),
                  pl.BlockSpec((1, d), lambda i: (0, 0)),
                  pl.BlockSpec((d, n), lambda i: (0, 0))],
        out_specs=[pl.BlockSpec((TOKEN_BLOCK, s), lambda i: (i, 0)) for s in splits],
        out_shape=[jax.ShapeDtypeStruct((t, s), dt) for s, dt in zip(splits, dtypes)],
        compiler_params=_cparams(1),
        name="inproj",
    )(x, mod_l, nw, w)


def _even_out_kernel(yssd_ref, z_ref, nw_ref, yret_ref, w_ref, x_ref, mod_ref, o_ref):
    d = x_ref.shape[-1]
    u = yssd_ref[...] * _silu(z_ref[...])
    un = u * lax.rsqrt(jnp.mean(u * u, axis=-1, keepdims=True) + NORM_EPS) * nw_ref[...]
    k1 = yssd_ref.shape[-1]
    acc = _dot(un.astype(BF16), w_ref[0:k1, :]) + _dot(yret_ref[...], w_ref[k1:, :])
    o_ref[...] = x_ref[...] + mod_ref[:, 2 * d:3 * d] * acc


def _even_out(yssd, z, nw, yret, w, x, mod_l, cond_map):
    t, d = x.shape
    nb = t // TOKEN_BLOCK
    row = lambda i: (i, 0)
    const = lambda i: (0, 0)
    return pl.pallas_call(
        _even_out_kernel,
        grid=(nb,),
        in_specs=[pl.BlockSpec((TOKEN_BLOCK, yssd.shape[1]), row),
                  pl.BlockSpec((TOKEN_BLOCK, z.shape[1]), row),
                  pl.BlockSpec((1, nw.shape[1]), const),
                  pl.BlockSpec((TOKEN_BLOCK, yret.shape[1]), row),
                  pl.BlockSpec(w.shape, const),
                  pl.BlockSpec((TOKEN_BLOCK, d), row),
                  pl.BlockSpec((None, 1, mod_l.shape[-1]), lambda i: (cond_map(i), 0, 0))],
        out_specs=pl.BlockSpec((TOKEN_BLOCK, d), row),
        out_shape=jax.ShapeDtypeStruct((t, d), F32),
        compiler_params=_cparams(1),
        name="even_out",
    )(yssd, z, nw, yret, w, x, mod_l)


def _odd_out_kernel(a_ref, w_ref, x_ref, mod_ref, o_ref):
    d = x_ref.shape[-1]
    o_ref[...] = x_ref[...] + mod_ref[:, 2 * d:3 * d] * _dot(a_ref[...], w_ref[...])


def _odd_out(a, w, x, mod_l, cond_map):
    t, d = x.shape
    nb = t // TOKEN_BLOCK
    row = lambda i: (i, 0)
    return pl.pallas_call(
        _odd_out_kernel,
        grid=(nb,),
        in_specs=[pl.BlockSpec((TOKEN_BLOCK, a.shape[1]), row),
                  pl.BlockSpec(w.shape, lambda i: (0, 0)),
                  pl.BlockSpec((TOKEN_BLOCK, d), row),
                  pl.BlockSpec((None, 1, mod_l.shape[-1]), lambda i: (cond_map(i), 0, 0))],
        out_specs=pl.BlockSpec((TOKEN_BLOCK, d), row),
        out_shape=jax.ShapeDtypeStruct((t, d), F32),
        compiler_params=_cparams(1),
        name="odd_out",
    )(a, w, x, mod_l)


def _tri(q, upper):
    r = lax.broadcasted_iota(jnp.int32, (q, q), 0)
    c = lax.broadcasted_iota(jnp.int32, (q, q), 1)
    return (c >= r) if upper else (c <= r)


def _prefix_suffix_sums(x):
    q = x.shape[0]
    m = _tri(q, upper=False).astype(BF16)
    pre = sum(_dot(m, part) for part in _split3(x))
    return pre, pre[q - 1:q, :] - pre + x


def _ssd_kernel(*refs, seq_len, has_h0, emit_state):
    it = iter(refs)
    x_ref, b_ref, c_ref = next(it), next(it), next(it)
    cwx_ref, cwb_ref, cwc_ref = next(it), next(it), next(it)
    cbx_ref, cbb_ref, cbc_ref = next(it), next(it), next(it)
    dt_ref, dtb_ref, alog_ref, dskip_ref = next(it), next(it), next(it), next(it)
    h0_ref = next(it) if has_h0 else None
    next(it)
    y_ref = next(it)
    st_ref = next(it) if emit_state else None

    L = seq_len
    q = SCAN_CHUNK
    nc = L // q
    pair = pl.program_id(1)
    half = SSD_HEAD_DIM

    row = lax.broadcasted_iota(jnp.int32, (L, LANES), 0)

    def conv_silu(ref, w_ref, bias_ref):
        x = ref[...]
        w = w_ref[...]
        prev = jnp.where(row == 0, 0.0, pltpu.roll(x, 1, axis=0))
        nxt = jnp.where(row == L - 1, 0.0, pltpu.roll(x, L - 1, axis=0))
        return _silu(prev * w[0:1, :] + x * w[1:2, :] + nxt * w[2:3, :] + bias_ref[...])

    xs = conv_silu(x_ref, cwx_ref, cbx_ref)
    bm = conv_silu(b_ref, cwb_ref, cbb_ref)
    cm = conv_silu(c_ref, cwc_ref, cbc_ref)
    xs_bf, bm_bf, cm_bf = xs.astype(BF16), bm.astype(BF16), cm.astype(BF16)

    dt = _softplus(dt_ref[...] + dtb_ref[...])
    la = -dt * jnp.exp(alog_ref[...])

    lane = lax.broadcasted_iota(jnp.int32, (q, LANES), 1)
    first = lane < half
    tril = _tri(q, upper=False)
    triu = _tri(q, upper=True)

    def pick_col(arr, col):
        return jnp.sum(jnp.where(lane == col, arr, 0.0), axis=1, keepdims=True)

    def pick_row(arr_t, r):
        rr = lax.broadcasted_iota(jnp.int32, arr_t.shape, 0)
        return jnp.sum(jnp.where(rr == r, arr_t, 0.0), axis=0, keepdims=True)

    y_parts, e_f, e_b, d_sf, d_sb, dec_f, dec_b = [], [], [], [], [], [], []
    for c in range(nc):
        sl = slice(c * q, (c + 1) * q)
        la_c, dt_c = la[sl], dt[sl]
        cum, rcs = _prefix_suffix_sums(la_c)
        cum_t, rcs_t, dt_t = cum.T, rcs.T, dt_c.T
        s_g = _dot_nt(cm_bf[sl], bm_bf[sl])
        y_c = None
        cols = {}
        for hh in range(2):
            head = 2 * pair + hh
            cf, cb = pick_col(cum, head), pick_col(rcs, SSD_HEADS + head)
            rf, rb = pick_row(cum_t, head), pick_row(rcs_t, SSD_HEADS + head)
            dtf_r, dtb_r = pick_row(dt_t, head), pick_row(dt_t, SSD_HEADS + head)
            dec = (jnp.exp(jnp.where(tril, cf - rf, NEG_BIG)) * dtf_r
                   + jnp.exp(jnp.where(triu, cb - rb, NEG_BIG)) * dtb_r)
            y_h = _dot((s_g * dec).astype(BF16), xs_bf[sl])
            y_c = y_h if hh == 0 else jnp.where(first, y_c, y_h)
            cols[hh] = (cf, cb, pick_col(dt_c, head), pick_col(dt_c, SSD_HEADS + head))
        cum_e = jnp.where(first, cols[0][0], cols[1][0])
        rcs_e = jnp.where(first, cols[0][1], cols[1][1])
        dtf_e = jnp.where(first, cols[0][2], cols[1][2])
        dtb_e = jnp.where(first, cols[0][3], cols[1][3])
        y_parts.append(y_c)
        e_f.append(jnp.exp(cum_e))
        e_b.append(jnp.exp(rcs_e))
        xs_c = xs[sl]
        wf = xs_c * dtf_e * jnp.exp(cum_e[q - 1:q, :] - cum_e)
        wb = xs_c * dtb_e * jnp.exp(rcs_e[0:1, :] - rcs_e)
        d_sf.append(_dot_tn(bm_bf[sl], wf.astype(BF16)))
        d_sb.append(_dot_tn(bm_bf[sl], wb.astype(BF16)))
        dec_f.append(jnp.exp(cum_e[q - 1:q, :]))
        dec_b.append(jnp.exp(rcs_e[0:1, :]))

    zero = jnp.zeros((SSD_STATE, LANES), F32)
    sf = h0_ref[0] if has_h0 else zero
    sf_in = []
    for c in range(nc):
        sf_in.append(sf)
        sf = dec_f[c] * sf + d_sf[c]
    sb = h0_ref[1] if has_h0 else zero
    sb_in = [None] * nc
    for c in reversed(range(nc)):
        sb_in[c] = sb
        sb = dec_b[c] * sb + d_sb[c]

    dsk = dskip_ref[...]
    for c in range(nc):
        sl = slice(c * q, (c + 1) * q)
        y_c = y_parts[c] + xs[sl] * dsk
        if has_h0 or c > 0:
            y_c = y_c + e_f[c] * _dot(cm_bf[sl], sf_in[c].astype(BF16))
        if has_h0 or c < nc - 1:
            y_c = y_c + e_b[c] * _dot(cm_bf[sl], sb_in[c].astype(BF16))
        y_ref[sl, :] = y_c
    if emit_state:
        st_ref[0] = sf
        st_ref[1] = sb


def _ssd_scan(xbc, dt_raw, conv_w, conv_b, dt_bias, a_log, d_skip, *, row0, n_seq, seq_len,
              h0, prev_out):
    t = xbc.shape[0]
    blk0 = row0 // seq_len
    n_pairs = SSD_HEADS // 2
    pairs_per_group = n_pairs // SSD_GROUPS
    xoff, boff, coff = 0, SSD_INNER // LANES, (SSD_INNER + SSD_GROUPS * SSD_STATE) // LANES
    has_h0 = h0 is not None
    emit_state = not has_h0

    def seq_blk(col_fn):
        return lambda s, p: (blk0 + s, col_fn(p))

    fx = lambda p: xoff + p
    fb = lambda p: boff + p // pairs_per_group
    fc = lambda p: coff + p // pairs_per_group
    vec = lambda rows, col_fn: pl.BlockSpec((rows, LANES), lambda s, p: (0, col_fn(p)))
    in_specs = [pl.BlockSpec((seq_len, LANES), seq_blk(fx)),
                pl.BlockSpec((seq_len, LANES), seq_blk(fb)),
                pl.BlockSpec((seq_len, LANES), seq_blk(fc)),
                vec(3, fx), vec(3, fb), vec(3, fc),
                vec(1, fx), vec(1, fb), vec(1, fc),
                pl.BlockSpec((seq_len, LANES), lambda s, p: (blk0 + s, 0)),
                pl.BlockSpec((1, LANES), lambda s, p: (0, 0)),
                pl.BlockSpec((1, LANES), lambda s, p: (0, 0)),
                vec(1, fx)]
    args = [xbc, xbc, xbc, conv_w, conv_w, conv_w, conv_b, conv_b, conv_b,
            dt_raw, dt_bias, a_log, d_skip]
    io_alias = {}
    if has_h0:
        in_specs.append(pl.BlockSpec((None, 2, None, SSD_STATE, LANES), lambda s, p: (s, 0, p, 0, 0)))
        args.append(h0)
    io_alias = {len(args): 0}
    in_specs.append(pl.BlockSpec(memory_space=pl.ANY))
    args.append(prev_out)
    out_specs = [pl.BlockSpec((seq_len, LANES), lambda s, p: (blk0 + s, p))]
    out_shape = [jax.ShapeDtypeStruct((t, SSD_INNER), F32)]
    if emit_state:
        out_specs.append(pl.BlockSpec((None, 2, None, SSD_STATE, LANES), lambda s, p: (s, 0, p, 0, 0)))
        out_shape.append(jax.ShapeDtypeStruct((n_seq, 2, n_pairs, SSD_STATE, LANES), F32))
    kern = functools.partial(_ssd_kernel, seq_len=seq_len, has_h0=has_h0, emit_state=emit_state)
    return pl.pallas_call(
        kern, grid=(n_seq, n_pairs), in_specs=in_specs, out_specs=out_specs, out_shape=out_shape,
        input_output_aliases=io_alias, compiler_params=_cparams(2), name="ssd_scan",
    )(*args)


def _ret_kernel(*refs, seq_len, has_h0, emit_state, use_rope):
    it = iter(refs)
    q_ref, k_ref, v_ref, g_ref, lg_ref = next(it), next(it), next(it), next(it), next(it)
    cos_ref = next(it) if use_rope else None
    sin_ref = next(it) if use_rope else None
    h0_ref = next(it) if has_h0 else None
    next(it)
    o_ref = next(it)
    st_ref = next(it) if emit_state else None

    L = seq_len
    qn = SCAN_CHUNK
    nc = L // qn
    head = pl.program_id(1)

    qf = q_ref[...]
    kf = k_ref[...] * (RET_QK_DIM ** -0.5)
    if use_rope:
        lane = lax.broadcasted_iota(jnp.int32, (L, LANES), 1)
        even = (lane % 2) == 0
        cos, sin = cos_ref[...], sin_ref[...]

        def rope(x):
            swapped = jnp.where(even, pltpu.roll(x, LANES - 1, axis=1), pltpu.roll(x, 1, axis=1))
            return x * cos + swapped * sin
        qf, kf = rope(qf), rope(kf)
    q_bf, k_bf = qf.astype(BF16), kf.astype(BF16)
    v_bf = v_ref[...]

    lg = _log_sigmoid(lg_ref[...])
    rr = lax.broadcasted_iota(jnp.int32, lg.shape, 0)
    lgf = jnp.sum(jnp.where(rr == head, lg, 0.0), axis=0, keepdims=True)[:, 0:1]
    lgb = jnp.sum(jnp.where(rr == RET_HEADS + head, lg, 0.0), axis=0, keepdims=True)[:, 0:1]

    ti = lax.broadcasted_iota(jnp.int32, (qn, qn), 0)
    si = lax.broadcasted_iota(jnp.int32, (qn, qn), 1)
    dist = (ti - si).astype(F32)
    dec = (jnp.exp(jnp.where(ti >= si, dist * lgf, NEG_BIG))
           + jnp.exp(jnp.where(si >= ti, -dist * lgb, NEG_BIG)))
    tcol = lax.broadcasted_iota(jnp.int32, (qn, 1), 0).astype(F32)
    ef = jnp.exp((tcol + 1.0) * lgf)
    eb = jnp.exp((qn - tcol) * lgb)
    wf = jnp.exp((qn - 1.0 - tcol) * lgf)
    wb = jnp.exp(tcol * lgb)
    dec_f = jnp.exp(qn * lgf)
    dec_b = jnp.exp(qn * lgb)

    y_parts, d_sf, d_sb = [], [], []
    for c in range(nc):
        sl = slice(c * qn, (c + 1) * qn)
        s = _dot_nt(q_bf[sl], k_bf[sl])
        y_parts.append(_dot((s * dec).astype(BF16), v_bf[sl]))
        d_sf.append(_dot_tn((kf[sl] * wf).astype(BF16), v_bf[sl]))
        d_sb.append(_dot_tn((kf[sl] * wb).astype(BF16), v_bf[sl]))

    zero = jnp.zeros((RET_QK_DIM, RET_V_DIM), F32)
    sf = h0_ref[0] if has_h0 else zero
    sf_in = []
    for c in range(nc):
        sf_in.append(sf)
        sf = dec_f * sf + d_sf[c]
    sb = h0_ref[1] if has_h0 else zero
    sb_in = [None] * nc
    for c in reversed(range(nc)):
        sb_in[c] = sb
        sb = dec_b * sb + d_sb[c]

    for c in range(nc):
        sl = slice(c * qn, (c + 1) * qn)
        o = y_parts[c]
        if has_h0 or c > 0:
            o = o + ef * _dot(q_bf[sl], sf_in[c].astype(BF16))
        if has_h0 or c < nc - 1:
            o = o + eb * _dot(q_bf[sl], sb_in[c].astype(BF16))
        mu = jnp.mean(o, axis=-1, keepdims=True)
        var = jnp.mean(jnp.square(o - mu), axis=-1, keepdims=True)
        on = (o - mu) * lax.rsqrt(var + NORM_EPS)
        o_ref[sl, :] = (on * _silu(g_ref[sl, :])).astype(o_ref.dtype)
    if emit_state:
        st_ref[0] = sf
        st_ref[1] = sb


def _ret_scan(q, k, v, g, lg_rows, rope, *, row0, n_seq, seq_len, h0, prev_out, layer_j):
    t = q.shape[0]
    blk0 = row0 // seq_len
    has_h0 = h0 is not None
    emit_state = not has_h0
    use_rope = rope is not None
    blk = lambda w: pl.BlockSpec((seq_len, w), lambda s, h: (blk0 + s, h))
    in_specs = [blk(RET_QK_DIM), blk(RET_QK_DIM), blk(RET_V_DIM), blk(RET_V_DIM),
                pl.BlockSpec(lg_rows.shape, lambda s, h: (0, 0))]
    args = [q, k, v, g, lg_rows]
    if use_rope:
        in_specs += [pl.BlockSpec((seq_len, LANES), lambda s, h: (0, 0))] * 2
        args += [rope[0], rope[1]]
    if has_h0:
        in_specs.append(pl.BlockSpec((None, None, 2, None, RET_QK_DIM, RET_V_DIM),
                                     lambda s, h: (s, layer_j, 0, h, 0, 0)))
        args.append(h0)
    io_alias = {}
    io_alias = {len(args): 0}
    in_specs.append(pl.BlockSpec(memory_space=pl.ANY))
    args.append(prev_out)
    out_specs = [pl.BlockSpec((seq_len, RET_V_DIM), lambda s, h: (blk0 + s, h))]
    out_shape = [jax.ShapeDtypeStruct((t, RET_V), BF16)]
    if emit_state:
        out_specs.append(pl.BlockSpec((None, 2, None, RET_QK_DIM, RET_V_DIM), lambda s, h: (s, 0, h, 0, 0)))
        out_shape.append(jax.ShapeDtypeStruct((n_seq, 2, RET_HEADS, RET_QK_DIM, RET_V_DIM), F32))
    kern = functools.partial(_ret_kernel, seq_len=seq_len, has_h0=has_h0, emit_state=emit_state,
                             use_rope=use_rope)
    return pl.pallas_call(
        kern, grid=(n_seq, RET_HEADS), in_specs=in_specs, out_specs=out_specs, out_shape=out_shape,
        input_output_aliases=io_alias, compiler_params=_cparams(2), name="ret_scan",
    )(*args)


def _pair_boundaries(block_last, block_first, upper, h):
    qn = block_last.shape[0]
    last_prev = pltpu.roll(block_last, h, axis=0)
    last_next = pltpu.roll(block_last, qn - h, axis=0)
    first_prev = pltpu.roll(block_first, h, axis=0)
    first_next = pltpu.roll(block_first, qn - h, axis=0)
    ref_f = jnp.where(upper, last_prev, block_last)
    ref_b = jnp.where(upper, block_first, first_next)
    return ref_f, ref_b, jnp.where(upper, block_last, last_next), jnp.where(upper, first_prev, block_first)


def _hgrn_kernel(*refs, seq_len, has_h0, emit_state):
    it = iter(refs)
    q_ref, ff_ref, fb_ref, i_ref, g_ref = next(it), next(it), next(it), next(it), next(it)
    llb_ref, l1m_ref, nw_ref = next(it), next(it), next(it)
    h0_ref = next(it) if has_h0 else None
    next(it)
    o_ref = next(it)
    st_ref = next(it) if emit_state else None

    L = seq_len
    qn = SCAN_CHUNK
    nc = L // qn

    def log_forget(fz, d):
        a = llb_ref[d:d + 1, :]
        b = l1m_ref[d:d + 1, :] + _log_sigmoid(fz)
        m = jnp.maximum(a, b)
        return m + jnp.log(1.0 + jnp.exp(-jnp.abs(a - b)))

    lf_f = log_forget(ff_ref[...], 0)
    lf_b = log_forget(fb_ref[...], 1)
    k_f = 1.0 - jnp.exp(lf_f)
    k_b = 1.0 - jnp.exp(lf_b)
    qv = q_ref[...]
    v_bf = i_ref[...]

    t_col = lax.broadcasted_iota(jnp.int32, (qn, 1), 0)
    ti = lax.broadcasted_iota(jnp.int32, (qn, qn), 0)
    si = lax.broadcasted_iota(jnp.int32, (qn, qn), 1)
    pair_level = 31 - lax.clz(ti ^ si)
    n = HGRN_HEAD_DIM

    y_parts, e_f, e_b, d_sf, d_sb, dec_f, dec_b = [], [], [], [], [], [], []
    for c in range(nc):
        sl = slice(c * qn, (c + 1) * qn)
        q_c, kf_c, kb_c = qv[sl], k_f[sl], k_b[sl]
        pre, suf = _prefix_suffix_sums(jnp.concatenate([lf_f[sl], lf_b[sl]], axis=1))
        cum, rcs = pre[:, :n], suf[:, n:]
        scores = jnp.zeros((qn, qn), F32)
        cum2, rcs2 = cum * LOG2E, rcs * LOG2E
        block_last, block_first = cum2, rcs2
        h = 1
        while h < qn:
            upper = ((t_col // h) % 2) == 1
            ref_f, ref_b, block_last, block_first = _pair_boundaries(block_last, block_first, upper, h)
            d_f, d_b = cum2 - ref_f, rcs2 - ref_b
            aq = jnp.exp2(jnp.where(upper, d_f, d_b))
            ak = jnp.exp2(-jnp.where(upper, d_b, d_f))
            qt = (q_c * aq).astype(BF16)
            kt = (jnp.where(upper, kb_c, kf_c) * ak).astype(BF16)
            s_l = _dot_nt(qt, kt)
            scores = jnp.where(pair_level == h.bit_length() - 1, s_l, scores)
            h *= 2
        diag = jnp.sum(q_c * (kf_c + kb_c), axis=-1, keepdims=True)
        y_parts.append(_dot(scores.astype(BF16), v_bf[sl]) + diag * v_bf[sl].astype(F32))
        e_f.append(jnp.exp(cum))
        e_b.append(jnp.exp(rcs))
        d_sf.append(_dot_tn((kf_c * jnp.exp(cum[qn - 1:qn, :] - cum)).astype(BF16), v_bf[sl]))
        d_sb.append(_dot_tn((kb_c * jnp.exp(rcs[0:1, :] - rcs)).astype(BF16), v_bf[sl]))
        dec_f.append(jnp.broadcast_to(jnp.exp(cum[qn - 1:qn, :]), (n, n)).T)
        dec_b.append(jnp.broadcast_to(jnp.exp(rcs[0:1, :]), (n, n)).T)

    zero = jnp.zeros((HGRN_HEAD_DIM, HGRN_HEAD_DIM), F32)
    sf = h0_ref[0] if has_h0 else zero
    sf_in = []
    for c in range(nc):
        sf_in.append(sf)
        sf = dec_f[c] * sf + d_sf[c]
    sb = h0_ref[1] if has_h0 else zero
    sb_in = [None] * nc
    for c in reversed(range(nc)):
        sb_in[c] = sb
        sb = dec_b[c] * sb + d_sb[c]

    nw = nw_ref[...]
    for c in range(nc):
        sl = slice(c * qn, (c + 1) * qn)
        o = y_parts[c]
        if has_h0 or c > 0:
            o = o + _dot((qv[sl] * e_f[c]).astype(BF16), sf_in[c].astype(BF16))
        if has_h0 or c < nc - 1:
            o = o + _dot((qv[sl] * e_b[c]).astype(BF16), sb_in[c].astype(BF16))
        on = o * lax.rsqrt(jnp.mean(o * o, axis=-1, keepdims=True) + NORM_EPS) * nw
        o_ref[sl, :] = (on * _silu(g_ref[sl, :])).astype(o_ref.dtype)
    if emit_state:
        st_ref[0] = sf
        st_ref[1] = sb


def _hgrn_scan(q, ff, fb, iv, g, log_lb, log1m_lb, nw, *, row0, n_seq, seq_len, h0, prev_out, layer_j):
    t = q.shape[0]
    blk0 = row0 // seq_len
    has_h0 = h0 is not None
    emit_state = not has_h0
    n = HGRN_HEAD_DIM
    blk = pl.BlockSpec((seq_len, n), lambda s, h: (blk0 + s, h))
    in_specs = [blk, blk, blk, blk, blk,
                pl.BlockSpec((2, n), lambda s, h: (0, h)),
                pl.BlockSpec((2, n), lambda s, h: (0, h)),
                pl.BlockSpec((1, n), lambda s, h: (0, h))]
    args = [q, ff, fb, iv, g, log_lb, log1m_lb, nw]
    if has_h0:
        in_specs.append(pl.BlockSpec((None, None, 2, None, n, n), lambda s, h: (s, layer_j, 0, h, 0, 0)))
        args.append(h0)
    io_alias = {}
    io_alias = {len(args): 0}
    in_specs.append(pl.BlockSpec(memory_space=pl.ANY))
    args.append(prev_out)
    out_specs = [pl.BlockSpec((seq_len, n), lambda s, h: (blk0 + s, h))]
    out_shape = [jax.ShapeDtypeStruct((t, HGRN_HEADS * n), BF16)]
    if emit_state:
        out_specs.append(pl.BlockSpec((None, 2, None, n, n), lambda s, h: (s, 0, h, 0, 0)))
        out_shape.append(jax.ShapeDtypeStruct((n_seq, 2, HGRN_HEADS, n, n), F32))
    kern = functools.partial(_hgrn_kernel, seq_len=seq_len, has_h0=has_h0, emit_state=emit_state)
    return pl.pallas_call(
        kern, grid=(n_seq, HGRN_HEADS), in_specs=in_specs, out_specs=out_specs, out_shape=out_shape,
        input_output_aliases=io_alias, compiler_params=_cparams(2), name="hgrn_scan",
    )(*args)


def _router_kernel(x_ref, mod_ref, nw_ref, rw_ref, rb_ref, h_ref, gate_ref, rank_ref, cnt_ref):
    h = _norm_mod(x_ref[...], nw_ref[...], mod_ref[...], 3, 4)
    h_ref[...] = h.astype(BF16)
    logits = jnp.dot(h, rw_ref[...], precision=HIGHEST, preferred_element_type=F32) + rb_ref[...]
    lane = lax.broadcasted_iota(jnp.int32, logits.shape, 1).astype(F32)
    work = logits
    vals, idxs = [], []
    for _ in range(TOP_K):
        m = jnp.max(work, axis=-1, keepdims=True)
        idx = jnp.min(jnp.where(work == m, lane, float(LANES)), axis=-1, keepdims=True)
        vals.append(m)
        idxs.append(idx)
        work = jnp.where(lane == idx, -jnp.inf, work)
    es = [jnp.exp(v - vals[0]) for v in vals]
    denom = es[0] + es[1] + es[2] + es[3]
    gates = jnp.zeros(logits.shape, F32)
    member = jnp.zeros(logits.shape, F32)
    for k in range(TOP_K):
        hit = lane == idxs[k]
        gates = jnp.where(hit, es[k] / denom, gates)
        member = jnp.where(hit, 1.0, member)
    tb = logits.shape[0]
    before = _tri(tb, upper=False) & ~_tri(tb, upper=True)
    rank = _dot(before.astype(BF16), member.astype(BF16))
    gate_ref[...] = gates
    rank_ref[...] = jnp.where(member > 0.0, rank, -1.0)
    cnt_ref[...] = jnp.sum(member, axis=0, keepdims=True).astype(jnp.int32)


def _router(x, mod_l, nw, rw_pad, rb_pad, cond_map):
    t, d = x.shape
    nb = t // TOKEN_BLOCK
    row = lambda i: (i, 0)
    const = lambda i: (0, 0)
    return pl.pallas_call(
        _router_kernel,
        grid=(nb,),
        in_specs=[pl.BlockSpec((TOKEN_BLOCK, d), row),
                  pl.BlockSpec((None, 1, mod_l.shape[-1]), lambda i: (cond_map(i), 0, 0)),
                  pl.BlockSpec((1, d), const),
                  pl.BlockSpec((d, LANES), const),
                  pl.BlockSpec((1, LANES), const)],
        out_specs=[pl.BlockSpec((TOKEN_BLOCK, d), row),
                   pl.BlockSpec((TOKEN_BLOCK, LANES), row),
                   pl.BlockSpec((TOKEN_BLOCK, LANES), row),
                   pl.BlockSpec((None, 1, LANES), lambda i: (i, 0, 0))],
        out_shape=[jax.ShapeDtypeStruct((t, d), BF16),
                   jax.ShapeDtypeStruct((t, LANES), F32),
                   jax.ShapeDtypeStruct((t, LANES), F32),
                   jax.ShapeDtypeStruct((nb, 1, LANES), jnp.int32)],
        compiler_params=_cparams(1),
        name="router",
    )(x, mod_l, nw, rw_pad, rb_pad)


def _split2(x):
    hi = x.astype(BF16)
    return hi, (x - hi.astype(F32)).astype(BF16)


def _split3(x):
    hi, mid = _split2(x)
    r2 = x - hi.astype(F32) - mid.astype(F32)
    return hi, mid, r2.astype(BF16)


def _packed_owner(lo_row, hi_row):
    r = lax.broadcasted_iota(jnp.int32, (PACK_ROWS, LANES), 0).astype(F32)
    return (r >= lo_row) & (r < hi_row)


def _chunks(units, bits):
    for b in range(bits):
        yield (((units >> b) & 1) == 1,
               pl.multiple_of(((units >> (b + 1)) << (b + 1)) * ROW_ALIGN, ROW_ALIGN),
               ROW_ALIGN << b)


def _group_chunks(i, e, start_ref, lo_ref, units_ref):
    idx = i * N_EXPERTS + e
    for present, base, rows in _chunks(units_ref[idx], GROUP_CHUNK_BITS):
        yield (present, pl.multiple_of(lo_ref[idx] + base, ROW_ALIGN),
               pl.multiple_of(start_ref[idx] + base, ROW_ALIGN), rows)


def _zero_unused_rows(tail_start_ref, tail_units_ref, nu_ref, xs_hbm, zeros_buf, sem):
    n_tiles = xs_hbm.shape[0] // EXPERT_TILE
    zeros_buf[...] = jnp.zeros_like(zeros_buf)

    def tail_copy(off, rows):
        return pltpu.make_async_copy(zeros_buf.at[pl.ds(0, rows)], xs_hbm.at[pl.ds(off, rows)], sem)

    def tile_copy(n):
        return pltpu.make_async_copy(zeros_buf, xs_hbm.at[pl.ds(pl.multiple_of(n * EXPERT_TILE, EXPERT_TILE),
                                                                EXPERT_TILE)], sem)

    def each(fn):
        for e in range(N_EXPERTS):
            for present, base, rows in _chunks(tail_units_ref[e], TAIL_CHUNK_BITS):
                @pl.when(present)
                def _():
                    fn(tail_copy(pl.multiple_of(tail_start_ref[e] + base, ROW_ALIGN), rows))

        def body(n, carry):
            fn(tile_copy(n))
            return carry
        lax.fori_loop(nu_ref[0], n_tiles, body, 0)

    each(lambda c: c.start())
    each(lambda c: c.wait())


def _dispatch_kernel(start_ref, lo_ref, units_ref, tail_start_ref, tail_units_ref, nu_ref,
                     h_ref, rank_ref, lorow_ref, hirow_ref, xs_hbm, xs_buf, zeros_buf, sem):
    i = pl.program_id(0)

    @pl.when(i == 0)
    def _():
        _zero_unused_rows(tail_start_ref, tail_units_ref, nu_ref, xs_hbm, zeros_buf, sem.at[0])

    lo_row = lorow_ref[...]
    owner = _packed_owner(lo_row, hirow_ref[...])
    owned = jnp.sum(owner.astype(F32), axis=1, keepdims=True) > 0.0
    r = lax.broadcasted_iota(jnp.int32, (PACK_ROWS, 1), 0).astype(F32)
    local = r - jnp.sum(jnp.where(owner, lo_row, 0.0), axis=1, keepdims=True)
    rank_rows = _dot_nt(owner.astype(BF16), rank_ref[...].astype(BF16))
    sel = ((rank_rows == local) & owned).astype(BF16)
    slot = i % 2
    xs_buf[slot] = _dot(sel, h_ref[...]).astype(BF16)

    def each_copy(tile, fn):
        s = tile % 2
        for e in range(N_EXPERTS):
            for present, v_off, s_off, rows in _group_chunks(tile, e, start_ref, lo_ref, units_ref):
                @pl.when(present)
                def _():
                    fn(pltpu.make_async_copy(xs_buf.at[s, pl.ds(v_off, rows)], xs_hbm.at[pl.ds(s_off, rows)],
                                             sem.at[s]))

    each_copy(i, lambda c: c.start())

    @pl.when(i > 0)
    def _():
        each_copy(i - 1, lambda c: c.wait())

    @pl.when(i == pl.num_programs(0) - 1)
    def _():
        each_copy(i, lambda c: c.wait())


def _dispatch(hffn, rank, tables, n_rows):
    t, d = hffn.shape
    nb = t // TOKEN_BLOCK
    row = lambda i, *_: (i, 0)
    vec = pl.BlockSpec((None, 1, LANES), lambda i, *_: (i, 0, 0))
    grid_spec = pltpu.PrefetchScalarGridSpec(
        num_scalar_prefetch=6,
        grid=(nb,),
        in_specs=[pl.BlockSpec((TOKEN_BLOCK, d), row), pl.BlockSpec((TOKEN_BLOCK, LANES), row), vec, vec],
        out_specs=pl.BlockSpec(memory_space=pl.ANY),
        scratch_shapes=[pltpu.VMEM((2, PACK_ROWS, d), BF16), pltpu.VMEM((EXPERT_TILE, d), BF16),
                        pltpu.SemaphoreType.DMA((2,))],
    )
    return pl.pallas_call(
        _dispatch_kernel, grid_spec=grid_spec,
        out_shape=jax.ShapeDtypeStruct((n_rows, d), BF16),
        compiler_params=_cparams(1), name="dispatch",
    )(tables["start"], tables["lo"], tables["units"], tables["tail_start"], tables["tail_units"],
      tables["n_used"], hffn, rank, tables["lo_row"], tables["hi_row"])


def _expert_kernel(first_ref, count_ref, nu_ref, x_hbm, wg_ref, wu_ref, wd_ref, bg_ref, bu_ref, bd_ref,
                   y_hbm, xbuf, ybuf, wg_bf, wu_bf, wd_bf, sem_in, sem_out):
    e = pl.program_id(0)
    first, count, n_used = first_ref[e], count_ref[e], nu_ref[0]
    n_tiles = y_hbm.shape[0] // EXPERT_TILE
    x_slots, y_slots = xbuf.shape[0], ybuf.shape[0]

    def rows(tile):
        return pl.ds(pl.multiple_of(tile * EXPERT_TILE, EXPERT_TILE), EXPERT_TILE)

    def x_copy(tile):
        slot = tile % x_slots
        return pltpu.make_async_copy(x_hbm.at[rows(tile)], xbuf.at[slot], sem_in.at[slot])

    def y_copy(tile):
        slot = tile % y_slots
        return pltpu.make_async_copy(ybuf.at[slot], y_hbm.at[rows(tile)], sem_out.at[slot])

    @pl.when(e == 0)
    def _():
        for tile in range(X_AHEAD):
            @pl.when(tile < n_used)
            def _():
                x_copy(tile).start()

    wg_bf[...] = wg_ref[...].astype(BF16)
    wu_bf[...] = wu_ref[...].astype(BF16)
    wd_bf[...] = wd_ref[...].astype(BF16)

    def one_tile(tile, carry):
        @pl.when(tile + X_AHEAD < n_used)
        def _():
            x_copy(tile + X_AHEAD).start()

        x_copy(tile).wait()

        @pl.when(tile >= y_slots)
        def _():
            y_copy(tile - y_slots).wait()

        x = xbuf[tile % x_slots]
        gt = jnp.minimum(_dot(x, wg_bf[...]) + bg_ref[...], SWIGLU_LIMIT)
        up = jnp.clip(_dot(x, wu_bf[...]) + bu_ref[...], -SWIGLU_LIMIT, SWIGLU_LIMIT)
        act = gt * _sigmoid(SWIGLU_ALPHA * gt) * (up + 1.0)
        ybuf[tile % y_slots] = (_dot(act.astype(BF16), wd_bf[...]) + bd_ref[...]).astype(ybuf.dtype)
        y_copy(tile).start()
        return carry

    lax.fori_loop(first, first + count, one_tile, 0)

    @pl.when(e == pl.num_programs(0) - 1)
    def _():
        for back in range(y_slots, 0, -1):
            @pl.when(n_used >= back)
            def _():
                y_copy(n_used - back).wait()
        ybuf[0] = jnp.zeros(ybuf.shape[1:], ybuf.dtype)

        def zero_copy(tile):
            return pltpu.make_async_copy(ybuf.at[0], y_hbm.at[rows(tile)], sem_out.at[0])

        def start(tile, carry):
            zero_copy(tile).start()
            return carry

        def wait(tile, carry):
            zero_copy(tile).wait()
            return carry
        lax.fori_loop(n_used, n_tiles, start, 0)
        lax.fori_loop(n_used, n_tiles, wait, 0)


def _experts(x_sorted, seg_first, seg_count, n_used, wg, wu, wd, bg, bu, bd, layer):
    n_rows, d = x_sorted.shape
    n_experts, dff = wg.shape[1], wg.shape[-1]
    wspec = lambda a: pl.BlockSpec((None, None) + a.shape[2:], lambda e, *_: (layer, e, 0, 0))
    anyspec = pl.BlockSpec(memory_space=pl.ANY)
    grid_spec = pltpu.PrefetchScalarGridSpec(
        num_scalar_prefetch=3,
        grid=(n_experts,),
        in_specs=[anyspec, wspec(wg), wspec(wu), wspec(wd), wspec(bg), wspec(bu), wspec(bd)],
        out_specs=anyspec,
        scratch_shapes=[pltpu.VMEM((X_AHEAD + 1, EXPERT_TILE, d), BF16), pltpu.VMEM((2, EXPERT_TILE, d), BF16),
                        pltpu.VMEM((d, dff), BF16), pltpu.VMEM((d, dff), BF16), pltpu.VMEM((dff, d), BF16),
                        pltpu.SemaphoreType.DMA((X_AHEAD + 1,)), pltpu.SemaphoreType.DMA((2,))],
    )
    return pl.pallas_call(
        _expert_kernel, grid_spec=grid_spec,
        out_shape=jax.ShapeDtypeStruct((n_rows, d), BF16),
        compiler_params=_cparams(1), name="experts",
    )(seg_first, seg_count, n_used, x_sorted, wg, wu, wd, bg, bu, bd)


def _combine_kernel(start_ref, lo_ref, units_ref, y_hbm, rank_ref, gate_ref, lorow_ref, hirow_ref,
                    x_ref, mod_ref, fnw_ref, o_ref, buf, sem, *, final_norm):
    i = pl.program_id(0)
    d = x_ref.shape[-1]

    def each_copy(tile, fn):
        s = tile % 2
        for e in range(N_EXPERTS):
            for present, v_off, s_off, rows in _group_chunks(tile, e, start_ref, lo_ref, units_ref):
                @pl.when(present)
                def _():
                    fn(pltpu.make_async_copy(y_hbm.at[pl.ds(s_off, rows)], buf.at[s, pl.ds(v_off, rows)],
                                             sem.at[s]))

    @pl.when(i == 0)
    def _():
        buf[...] = jnp.zeros_like(buf)
        each_copy(i, lambda c: c.start())

    @pl.when(i + 1 < pl.num_programs(0))
    def _():
        each_copy(i + 1, lambda c: c.start())

    lo_row = lorow_ref[...]
    owner = _packed_owner(lo_row, hirow_ref[...]).astype(BF16)
    rank_cols = _dot_nt(rank_ref[...].astype(BF16), owner)
    lo8 = jnp.broadcast_to(lo_row, (SUBLANES, LANES))
    lo_cols = sum(_dot_nt(part, owner) for part in _split3(lo8))[0:1, :]
    owned = _dot_nt(jnp.ones((SUBLANES, LANES), BF16), owner)[0:1, :] > 0.0
    col = lax.broadcasted_iota(jnp.int32, (1, PACK_ROWS), 1).astype(F32)
    match = (rank_cols == col - lo_cols) & owned
    g_hi, g_lo = _split2(gate_ref[...])
    q_hi = jnp.where(match, _dot_nt(g_hi, owner), 0.0).astype(BF16)
    q_lo = jnp.where(match, _dot_nt(g_lo, owner), 0.0).astype(BF16)

    each_copy(i, lambda c: c.wait())
    y = buf[i % 2]
    x = x_ref[...] + mod_ref[:, 5 * d:6 * d] * (_dot(q_hi, y) + _dot(q_lo, y))
    if final_norm:
        x = x * lax.rsqrt(jnp.mean(x * x, axis=-1, keepdims=True) + NORM_EPS) * fnw_ref[...]
    o_ref[...] = x


def _combine(y_sorted, rank, gates, tables, x, mod_l, fnw, cond_map, final_norm):
    t, d = x.shape
    nb = t // TOKEN_BLOCK
    row = lambda i, *_: (i, 0)
    vec = pl.BlockSpec((None, 1, LANES), lambda i, *_: (i, 0, 0))
    grid_spec = pltpu.PrefetchScalarGridSpec(
        num_scalar_prefetch=3,
        grid=(nb,),
        in_specs=[pl.BlockSpec(memory_space=pl.ANY),
                  pl.BlockSpec((TOKEN_BLOCK, LANES), row),
                  pl.BlockSpec((TOKEN_BLOCK, LANES), row),
                  vec, vec,
                  pl.BlockSpec((TOKEN_BLOCK, d), row),
                  pl.BlockSpec((None, 1, mod_l.shape[-1]), lambda i, *_: (cond_map(i), 0, 0)),
                  pl.BlockSpec((1, d), lambda i, *_: (0, 0))],
        out_specs=pl.BlockSpec((TOKEN_BLOCK, d), row),
        scratch_shapes=[pltpu.VMEM((2, PACK_ROWS, d), BF16), pltpu.SemaphoreType.DMA((2,))],
    )
    return pl.pallas_call(
        functools.partial(_combine_kernel, final_norm=final_norm), grid_spec=grid_spec,
        out_shape=jax.ShapeDtypeStruct((t, d), F32),
        compiler_params=_cparams(1), name="combine",
    )(tables["start"], tables["lo"], tables["units"], y_sorted, rank, gates, tables["lo_row"],
      tables["hi_row"], x, mod_l, fnw)


def _routing_tables(cnt_tiles, n_tiles):
    nb = cnt_tiles.shape[0]
    c = cnt_tiles[:, 0, :N_EXPERTS]
    ca = ((c + ROW_ALIGN - 1) // ROW_ALIGN) * ROW_ALIGN
    lo = jnp.cumsum(ca, axis=1) - ca
    tot = ca.sum(axis=0)
    seg = ((tot + EXPERT_TILE - 1) // EXPERT_TILE) * EXPERT_TILE
    ends = jnp.cumsum(seg)
    offs = ends - seg
    start = offs[None, :] + jnp.cumsum(ca, axis=0) - ca
    i32 = lambda a: a.astype(jnp.int32)
    row = lambda a: jnp.pad(a.astype(F32), ((0, 0), (0, LANES - N_EXPERTS))).reshape(nb, 1, LANES)
    return dict(start=i32(start.reshape(-1)), lo=i32(lo.reshape(-1)), units=i32((ca // ROW_ALIGN).reshape(-1)),
                tail_start=i32(offs + tot), tail_units=i32((seg - tot) // ROW_ALIGN),
                n_used=i32((ends[-1] // EXPERT_TILE).reshape(1)),
                seg_first=i32(offs // EXPERT_TILE), seg_count=i32(seg // EXPERT_TILE),
                lo_row=row(lo), hi_row=row(lo + ca))


def _rope_tables(length):
    rows = length // GRID_W
    r = jnp.broadcast_to(jnp.arange(rows, dtype=F32)[:, None], (rows, GRID_W)).reshape(-1)
    col = jnp.broadcast_to(jnp.arange(GRID_W, dtype=F32)[None, :], (rows, GRID_W)).reshape(-1)
    n_freq = RET_QK_DIM // 4
    inv = ROPE_BASE ** (-jnp.arange(n_freq, dtype=F32) / n_freq)
    ang = jnp.concatenate([r[:, None] * inv, col[:, None] * inv], axis=-1)
    cos = jnp.repeat(jnp.cos(ang), 2, axis=-1)
    sin = jnp.repeat(jnp.sin(ang), 2, axis=-1)
    sign = jnp.where(jnp.arange(RET_QK_DIM) % 2 == 0, -1.0, 1.0).astype(F32)
    return cos, sin * sign


def _pack_pairs(s):
    lead = s.shape[:-3]
    h, n, p = s.shape[-3:]
    s = s.reshape(lead + (h // 2, 2, n, p))
    s = jnp.moveaxis(s, -3, -2)
    return s.reshape(lead + (h // 2, n, 2 * p))


def _unpack_pairs(s):
    lead = s.shape[:-3]
    hp, n, p2 = s.shape[-3:]
    s = s.reshape(lead + (hp, n, 2, p2 // 2))
    s = jnp.moveaxis(s, -2, -3)
    return s.reshape(lead + (hp * 2, n, p2 // 2))


def kernel(x_prompt, x_sample, state_ssd, state_ret, state_hgrn, c, c_ctx, mod_w, mod_b, norm_mix_w, norm_ffn_w, even_in_w, ssd_conv_w, ssd_conv_b, ssd_dt_bias, ssd_a_log, ssd_d, ssd_norm_w, ret_decay_logit, even_out_w, odd_in_w, hgrn_lower_bound, hgrn_norm_w, odd_out_w, router_w, router_b, exp_w_gate, exp_b_gate, exp_w_up, exp_b_up, exp_w_down, exp_b_down, final_norm_w):
    bp, lp, d = x_prompt.shape
    bs, ls, _ = x_sample.shape
    depth = mod_w.shape[0]
    tp, ts = bp * lp, bs * ls
    t = tp + ts
    assert lp == TOKEN_BLOCK and ls % TOKEN_BLOCK == 0 and tp % ls == 0 and d == D_MODEL
    assert 1 + bs <= COND_ROWS
    prompt_blocks = tp // TOKEN_BLOCK
    cond_map = functools.partial(_cond_row, prompt_blocks=prompt_blocks, blocks_per_request=ls // TOKEN_BLOCK)

    x = jnp.concatenate([x_prompt.reshape(tp, d), x_sample.reshape(ts, d)], axis=0)
    cond = jnp.zeros((COND_ROWS, d), F32).at[0].set(c_ctx).at[1:1 + bs].set(c)
    mod = _modulation(cond, mod_w, mod_b)
    mod = mod.reshape(depth, COND_ROWS, 1, 6 * d)

    rope = _rope_tables(ls)
    n_tiles = -(-(t * TOP_K + (t // TOKEN_BLOCK) * N_EXPERTS * (ROW_ALIGN - 1)
                  + N_EXPERTS * (EXPERT_TILE - 1)) // EXPERT_TILE)

    lb = jnp.cumsum(jax.nn.softmax(hgrn_lower_bound.astype(F32), axis=0), axis=0)
    lb = lb - lb[0]
    log_lb = jnp.log(lb)
    log1m_lb = jnp.log1p(-lb)

    new_ssd, new_ret, new_hgrn = [], [], []
    for layer in range(depth):
        j = layer // 2
        mod_l = mod[layer]
        nw_mix = norm_mix_w[layer].reshape(1, d)
        if layer % 2 == 0:
            w = even_in_w[j]
            o = 0
            cols = {}
            for name, n in (("z", SSD_INNER), ("xbc", SSD_CONV_CH), ("dt", 2 * SSD_HEADS), ("q", RET_QK),
                            ("k", RET_QK), ("v", RET_V), ("g", RET_V)):
                cols[name] = w[:, o:o + n]
                o += n
            dt_pad = jnp.pad(cols["dt"], ((0, 0), (0, LANES - 2 * SSD_HEADS)))
            w_r = jnp.concatenate([cols["z"], cols["xbc"], cols["q"], cols["k"], cols["v"], cols["g"], dt_pad],
                                  axis=1).astype(BF16)
            z, xbc, q, k, v, g, dt_raw = _inproj(
                x, mod_l, nw_mix, w_r,
                (SSD_INNER, SSD_CONV_CH, RET_QK, RET_QK, RET_V, RET_V, LANES),
                (F32, F32, F32, F32, BF16, F32, F32), cond_map)
            pad32 = lambda a: jnp.pad(a.reshape(1, -1).astype(F32), ((0, 0), (0, LANES - 2 * SSD_HEADS)))
            ssd_args = (xbc, dt_raw, ssd_conv_w[j], ssd_conv_b[j].reshape(1, -1), pad32(ssd_dt_bias[j]),
                        pad32(ssd_a_log[j]), jnp.repeat(ssd_d[j], SSD_HEAD_DIM).reshape(1, -1))
            y_ssd, st_ssd = _ssd_scan(*ssd_args, row0=0, n_seq=bp, seq_len=lp, h0=None,
                                      prev_out=jnp.zeros((t, SSD_INNER), F32))
            h0_ssd = _pack_pairs(state_ssd[:, j])
            (y_ssd,) = _ssd_scan(*ssd_args, row0=tp, n_seq=bs, seq_len=ls, h0=h0_ssd, prev_out=y_ssd)
            new_ssd.append(_unpack_pairs(st_ssd))
            lg_rows = jnp.broadcast_to(ret_decay_logit[j].reshape(2 * RET_HEADS, 1).astype(F32),
                                       (2 * RET_HEADS, LANES))
            y_ret, st_ret = _ret_scan(q, k, v, g, lg_rows, None, row0=0, n_seq=bp, seq_len=lp, h0=None,
                                      prev_out=jnp.zeros((t, RET_V), BF16), layer_j=j)
            (y_ret,) = _ret_scan(q, k, v, g, lg_rows, rope, row0=tp, n_seq=bs, seq_len=ls, h0=state_ret,
                                 prev_out=y_ret, layer_j=j)
            new_ret.append(st_ret)
            x = _even_out(y_ssd, z, ssd_norm_w[j].reshape(1, -1), y_ret, even_out_w[j].astype(BF16), x, mod_l,
                          cond_map)
        else:
            w_r = odd_in_w[j].astype(BF16)
            n = HGRN_HEADS * HGRN_HEAD_DIM
            q, ff, fb, iv, g = _inproj(x, mod_l, nw_mix, w_r, (n,) * 5, (F32, F32, F32, BF16, F32), cond_map)
            hargs = (q, ff, fb, iv, g, log_lb[j], log1m_lb[j], hgrn_norm_w[j].reshape(1, -1))
            o_h, st_h = _hgrn_scan(*hargs, row0=0, n_seq=bp, seq_len=lp, h0=None,
                                   prev_out=jnp.zeros((t, n), BF16), layer_j=j)
            (o_h,) = _hgrn_scan(*hargs, row0=tp, n_seq=bs, seq_len=ls, h0=state_hgrn, prev_out=o_h, layer_j=j)
            new_hgrn.append(st_h)
            x = _odd_out(o_h, odd_out_w[j].astype(BF16), x, mod_l, cond_map)

        rw_pad = jnp.pad(router_w[layer], ((0, 0), (0, LANES - N_EXPERTS)))
        rb_pad = jnp.pad(router_b[layer].reshape(1, -1), ((0, 0), (0, LANES - N_EXPERTS)), constant_values=NEG_BIG)
        hffn, gates, rank, cnt_tiles = _router(x, mod_l, norm_ffn_w[layer].reshape(1, d), rw_pad, rb_pad, cond_map)
        tables = _routing_tables(cnt_tiles, n_tiles)
        x_sorted = _dispatch(hffn, rank, tables, n_tiles * EXPERT_TILE)
        e, dff = exp_b_gate.shape[1], exp_b_gate.shape[2]
        y_sorted = _experts(x_sorted, tables["seg_first"], tables["seg_count"], tables["n_used"], exp_w_gate,
                            exp_w_up, exp_w_down,
                            exp_b_gate.reshape(depth, e, 1, dff), exp_b_up.reshape(depth, e, 1, dff),
                            exp_b_down.reshape(depth, e, 1, d), layer)
        x = _combine(y_sorted, rank, gates, tables, x, mod_l, final_norm_w.reshape(1, d), cond_map,
                     final_norm=(layer == depth - 1))

    y_prompt = x[:tp].reshape(bp, lp, d)
    y_sample = x[tp:].reshape(bs, ls, d)
    return (y_prompt, y_sample, jnp.stack(new_ssd, axis=1), jnp.stack(new_ret, axis=1),
            jnp.stack(new_hgrn, axis=1))
```
